```python
import math
import jax, jax.numpy as jnp
from jax import lax
import numpy as np


D_MODEL = 2048
BATCH = 2
SEQ = 16384
DEPTH = 2

GRID_W = 64
CTX_LEN = 256
HEAD_DIM = 64
GROUP_WIDTH = D_MODEL // 4
MIX_WIDTH = 4 * GROUP_WIDTH
N_HEADS = GROUP_WIDTH // HEAD_DIM
KV_HEADS = 2
KV_WIDTH = KV_HEADS * HEAD_DIM
BLOCK = 128
WINDOW = 128
RET_CHUNK = 128
SSD_HEAD_DIM = 64
SSD_HEADS = GROUP_WIDTH // SSD_HEAD_DIM
SSD_STATE = 128
SSD_GROUPS = 2
SSD_BC = SSD_GROUPS * SSD_STATE
SSD_CONV = 5
SSD_CONV_CH = GROUP_WIDTH + 2 * SSD_BC
SSD_CHUNK = 128
FFN_DIM = ((8 * D_MODEL // 3 + 255) // 256) * 256
FFN_CONV = 3
ROPE_THETA = 10000.0
RET_THETA = 10000.0
EPS = 1e-6
NEG_INF = -1e30
IN_SPLITS = (GROUP_WIDTH, KV_WIDTH, KV_WIDTH,
             GROUP_WIDTH, KV_WIDTH, KV_WIDTH,
             GROUP_WIDTH, GROUP_WIDTH, GROUP_WIDTH, GROUP_WIDTH,
             GROUP_WIDTH, GROUP_WIDTH + 2 * SSD_BC, SSD_HEADS)
IN_COLS = 8 * GROUP_WIDTH + 4 * KV_WIDTH + 2 * SSD_BC + SSD_HEADS

kernel_name = 'hybrid_parallel_groups_dit_block'


def rmsnorm(x, g):
    xf = x.astype(jnp.float32)
    xf = xf * lax.rsqrt(jnp.mean(xf * xf, axis=-1, keepdims=True) + EPS)
    return xf.astype(x.dtype) * g


def heads(t, n):
    return t.reshape(t.shape[:-1] + (n, t.shape[-1] // n))


def flip(t):
    return jnp.flip(t, axis=1)


def split_columns(p):
    out, start = [], 0
    for width in IN_SPLITS:
        out.append(p[..., start:start + width])
        start += width
    return out


def rope_rotate(x, ang):
    n = ang.shape[-1]
    cos = jnp.cos(ang)[None, :, None, :].astype(x.dtype)
    sin = jnp.sin(ang)[None, :, None, :].astype(x.dtype)
    x1, x2 = x[..., :n], x[..., n:]
    return jnp.concatenate([x1 * cos - x2 * sin, x2 * cos + x1 * sin], axis=-1)


def rope_2d(x, ang_row, ang_col):
    half = x.shape[-1] // 2
    return jnp.concatenate([rope_rotate(x[..., :half], ang_row),
                            rope_rotate(x[..., half:], ang_col)], axis=-1)


def dwconv_centred(x, w, b):
    ch = x.shape[-1]
    y = lax.conv_general_dilated(x, w[:, None, :].astype(x.dtype), window_strides=(1,), padding='SAME',
                                 dimension_numbers=('NWC', 'WIO', 'NWC'), feature_group_count=ch)
    return y + b


def softmax_with_sink(s, sink):
    sink_col = jnp.broadcast_to(sink, s.shape[:-1] + (1,))
    return jax.nn.softmax(jnp.concatenate([s, sink_col], axis=-1), axis=-1)[..., :-1]


def gqa_attend(q, k, v, sink=None):
    b, lq, h, hd = q.shape
    kvh = k.shape[2]
    qg = q.reshape(b, lq, kvh, h // kvh, hd)
    s = jnp.einsum('bqkgd,bskd->bkgqs', qg, k).astype(jnp.float32)
    if sink is None:
        p = jax.nn.softmax(s, axis=-1)
    else:
        p = softmax_with_sink(s, sink.reshape(kvh, h // kvh)[None, :, :, None, None].astype(jnp.float32))
    o = jnp.einsum('bkgqs,bskd->bqkgd', p.astype(v.dtype), v)
    return o.reshape(b, lq, h * hd)


def dense_block_attention(q, k_all, v_all):
    b, l, h, hd = q.shape
    kvh = k_all.shape[2]
    nb = l // BLOCK
    qb = jnp.moveaxis(q.reshape(b, nb, BLOCK, kvh, h // kvh, hd), 1, 0)

    def one_block(q_blk):
        s = jnp.einsum('bqkgd,bskd->bkgqs', q_blk, k_all).astype(jnp.float32)
        p = jax.nn.softmax(s, axis=-1).astype(v_all.dtype)
        return jnp.einsum('bkgqs,bskd->bqkgd', p, v_all)

    o = lax.map(one_block, qb)
    return jnp.moveaxis(o, 0, 1).reshape(b, l, h * hd)


def banded_window_attention(q, k, v, kc, vc, sink):
    b, l, h, hd = q.shape
    kvh = k.shape[2]
    g = h // kvh
    nb = l // BLOCK
    qb = jnp.moveaxis(q.reshape(b, nb, BLOCK, kvh, g, hd), 1, 0)

    def bands(t):
        tp = jnp.pad(t, ((0, 0), (BLOCK, BLOCK), (0, 0), (0, 0))).reshape(b, nb + 2, BLOCK, kvh, hd)
        return jnp.moveaxis(jnp.concatenate([tp[:, :-2], tp[:, 1:-1], tp[:, 2:]], axis=2), 1, 0)

    kb, vb = bands(k), bands(v)
    qi = jnp.arange(BLOCK)
    kj = jnp.arange(3 * BLOCK)
    rel = kj[None, :] - BLOCK - qi[:, None]
    kpos = jnp.arange(nb)[:, None] * BLOCK - BLOCK + kj[None, :]
    mask = (jnp.abs(rel) <= WINDOW)[None] & ((kpos >= 0) & (kpos < l))[:, None, :]
    sink_b = sink.reshape(kvh, g)[None, :, :, None, None].astype(jnp.float32)

    def one_block(args):
        q_blk, k_blk, v_blk, m = args
        s_band = jnp.where(m, jnp.einsum('bqkgd,bskd->bkgqs', q_blk, k_blk).astype(jnp.float32), NEG_INF)
        s_ctx = jnp.einsum('bqkgd,bskd->bkgqs', q_blk, kc).astype(jnp.float32)
        p = softmax_with_sink(jnp.concatenate([s_band, s_ctx], axis=-1), sink_b).astype(v.dtype)
        nband = k_blk.shape[1]
        return (jnp.einsum('bkgqs,bskd->bqkgd', p[..., :nband], v_blk)
                + jnp.einsum('bkgqs,bskd->bqkgd', p[..., nband:], vc))

    o = lax.map(one_block, (qb, kb, vb, mask))
    return jnp.moveaxis(o, 0, 1).reshape(b, l, h * hd)


def retention_chunked(q, k, v, log_decay, s0):
    b, l, h, d = q.shape
    c = RET_CHUNK
    nc = l // c
    qc = q.reshape(b, nc, c, h, d)
    kc = k.reshape(b, nc, c, h, d)
    vc = v.reshape(b, nc, c, h, d)
    idx = jnp.arange(c, dtype=jnp.float32)
    rel = idx[:, None] - idx[None, :]
    intra = jnp.where(rel[None] >= 0, jnp.exp(jnp.maximum(rel, 0.0)[None] * log_decay[:, None, None]), 0.0)
    s = jnp.einsum('bnihd,bnjhd->bnhij', qc, kc) * intra.astype(q.dtype)
    o_intra = jnp.einsum('bnhij,bnjhe->bnihe', s, vc)
    k_end = kc * jnp.exp((c - 1 - idx)[:, None] * log_decay)[..., None].astype(q.dtype)
    kv = jnp.einsum('bnjhd,bnjhe->bnhde', k_end, vc)
    chunk_decay = jnp.exp(c * log_decay)[:, None, None]

    def step(state, kv_c):
        return state * chunk_decay + kv_c, state

    s_final, s_prev = lax.scan(step, s0, jnp.moveaxis(kv, 1, 0))
    s_prev = jnp.moveaxis(s_prev, 0, 1)
    q_start = qc * jnp.exp((idx + 1)[:, None] * log_decay)[..., None].astype(q.dtype)
    o_inter = jnp.einsum('bnihd,bnhde->bnihe', q_start, s_prev)
    return (o_intra + o_inter).reshape(b, l, h, d), s_final


def head_groupnorm(o, g):
    of = o.astype(jnp.float32)
    mu = jnp.mean(of, axis=-1, keepdims=True)
    var = jnp.mean(jnp.square(of - mu), axis=-1, keepdims=True)
    y = (of - mu) * lax.rsqrt(var + EPS)
    return y.reshape(o.shape[:-2] + (-1,)).astype(g.dtype) * g


def ssd_chunked(x, dt, a_coef, bm, cm, h0):
    b, l, h, p = x.shape
    g, n = bm.shape[2], bm.shape[3]
    hg = h // g
    c = SSD_CHUNK
    nc = l // c
    xc = x.reshape(b, nc, c, g, hg, p)
    dtc = dt.reshape(b, nc, c, g, hg)
    bc = bm.reshape(b, nc, c, g, n)
    cc = cm.reshape(b, nc, c, g, n)
    a_cum = jnp.cumsum(dtc * a_coef.reshape(g, hg), axis=2)
    lower = jnp.tril(jnp.ones((c, c), dtype=bool))
    seg = a_cum[:, :, :, None] - a_cum[:, :, None, :]
    decay = jnp.exp(jnp.where(lower[:, :, None, None], seg, -jnp.inf))
    cb = jnp.einsum('bnigs,bnjgs->bnijg', cc, bc)
    w = cb[..., None] * decay * dtc[:, :, None]
    y_intra = jnp.einsum('bnijgh,bnjghp->bnighp', w.astype(x.dtype), xc)
    to_end = jnp.exp(a_cum[:, :, -1:] - a_cum) * dtc
    states = jnp.einsum('bnjgs,bnjgh,bnjghp->bnghps', bc, to_end.astype(x.dtype), xc)
    chunk_decay = jnp.exp(a_cum[:, :, -1])

    def step(state, inp):
        st, dec = inp
        return state * dec[..., None, None] + st, state

    h_final, h_prev = lax.scan(step, h0, (jnp.moveaxis(states, 1, 0), jnp.moveaxis(chunk_decay, 1, 0)))
    h_prev = jnp.moveaxis(h_prev, 0, 1)
    y_inter = jnp.einsum('bnigs,bnigh,bnghps->bnighp', cc, jnp.exp(a_cum), h_prev)
    return (y_intra + y_inter).reshape(b, l, h, p), h_final


def token_mixers(p, pc, ang_row, ang_col, ang_ret, attn_qn_g, attn_kn_g, win_sink, ret_decay_logit, ret_gn_g,
                 ssd_conv_w, ssd_conv_b, ssd_a_log, ssd_dt_bias, ssd_d, ssd_norm_g, need_ctx):
    scale = HEAD_DIM ** -0.5
    b = p.shape[0]
    (aq, ak, av, wq, wk, wv, rq, rk, rv, rg, sz, sxbc, sdt) = split_columns(p)
    (aqc, akc, avc, wqc, wkc, wvc, rqc, rkc, rvc, rgc, szc, sxbcc, sdtc) = split_columns(pc)

    q_a = rope_2d(rmsnorm(heads(aq, N_HEADS), attn_qn_g), ang_row, ang_col) * scale
    k_a = rope_2d(rmsnorm(heads(ak, KV_HEADS), attn_kn_g), ang_row, ang_col)
    k_ac = rmsnorm(heads(akc, KV_HEADS), attn_kn_g)
    v_ac = heads(avc, KV_HEADS)
    o_att = dense_block_attention(q_a, jnp.concatenate([k_a, k_ac], axis=1),
                                  jnp.concatenate([heads(av, KV_HEADS), v_ac], axis=1))

    q_w = rope_2d(heads(wq, N_HEADS), ang_row, ang_col) * scale
    k_w = rope_2d(heads(wk, KV_HEADS), ang_row, ang_col)
    k_wc = heads(wkc, KV_HEADS)
    v_wc = heads(wvc, KV_HEADS)
    o_win = banded_window_attention(q_w, k_w, heads(wv, KV_HEADS), k_wc, v_wc, win_sink)

    log_dec = jax.nn.log_sigmoid(ret_decay_logit.astype(jnp.float32))
    q_r = rope_rotate(heads(rq, N_HEADS), ang_ret)
    k_r = rope_rotate(heads(rk, N_HEADS), ang_ret) * scale
    v_r = heads(rv, N_HEADS)
    q_rc, k_rc, v_rc = heads(rqc, N_HEADS), heads(rkc, N_HEADS) * scale, heads(rvc, N_HEADS)
    s0 = jnp.zeros((b, N_HEADS, HEAD_DIM, HEAD_DIM), jnp.float32)
    oc_f, sc_f = retention_chunked(q_rc, k_rc, v_rc, log_dec[0], s0)
    oc_b, sc_b = retention_chunked(flip(q_rc), flip(k_rc), flip(v_rc), log_dec[1], s0)
    o_f, _ = retention_chunked(q_r, k_r, v_r, log_dec[0], sc_f)
    o_b, _ = retention_chunked(flip(q_r), flip(k_r), flip(v_r), log_dec[1], sc_b)
    o_ret = head_groupnorm(o_f + flip(o_b), ret_gn_g) * jax.nn.silu(rg)

    a_coef = -jnp.exp(ssd_a_log.astype(jnp.float32))

    def ssd_inputs(xbc_raw, dt_raw):
        xbc = jax.nn.silu(dwconv_centred(xbc_raw, ssd_conv_w, ssd_conv_b))
        xs = heads(xbc[..., :GROUP_WIDTH], SSD_HEADS)
        bm = heads(xbc[..., GROUP_WIDTH:GROUP_WIDTH + SSD_BC], SSD_GROUPS)
        cm = heads(xbc[..., GROUP_WIDTH + SSD_BC:], SSD_GROUPS)
        dt = jax.nn.softplus(dt_raw.astype(jnp.float32)[None] + ssd_dt_bias.astype(jnp.float32)[:, None, None, :])
        return xs, bm, cm, dt

    def ssd_out(y_fwd, y_bwd_rev, xs, z):
        y = y_fwd + flip(y_bwd_rev) + ssd_d[:, None] * xs
        y = y.reshape(z.shape).astype(z.dtype) * jax.nn.silu(z)
        return rmsnorm(heads(y, SSD_GROUPS), ssd_norm_g.reshape(SSD_GROUPS, -1)).reshape(z.shape)

    xs, bm, cm, dt = ssd_inputs(sxbc, sdt)
    xsc, bmc, cmc, dtc = ssd_inputs(sxbcc, sdtc)
    h0 = jnp.zeros((b, SSD_GROUPS, SSD_HEADS // SSD_GROUPS, SSD_HEAD_DIM, SSD_STATE), jnp.float32)
    yc_f, hc_f = ssd_chunked(xsc, dtc[0], a_coef[0], bmc, cmc, h0)
    yc_b, hc_b = ssd_chunked(flip(xsc), flip(dtc[1]), a_coef[1], flip(bmc), flip(cmc), h0)
    y_f, _ = ssd_chunked(xs, dt[0], a_coef[0], bm, cm, hc_f)
    y_b, _ = ssd_chunked(flip(xs), flip(dt[1]), a_coef[1], flip(bm), flip(cm), hc_b)
    o_ssd = ssd_out(y_f, y_b, xs, sz)

    mix = jnp.concatenate([o_att.astype(p.dtype), o_win.astype(p.dtype), o_ret.astype(p.dtype), o_ssd.astype(p.dtype)], axis=-1)
    if not need_ctx:
        return mix, None
    q_ac = rmsnorm(heads(aqc, N_HEADS), attn_qn_g) * scale
    o_att_c = gqa_attend(q_ac, k_ac, v_ac)
    o_win_c = gqa_attend(heads(wqc, N_HEADS) * scale, k_wc, v_wc, win_sink)
    o_ret_c = head_groupnorm(oc_f + flip(oc_b), ret_gn_g) * jax.nn.silu(rgc)
    o_ssd_c = ssd_out(yc_f, yc_b, xsc, szc)
    mix_c = jnp.concatenate([o_att_c.astype(p.dtype), o_win_c.astype(p.dtype), o_ret_c.astype(p.dtype), o_ssd_c.astype(p.dtype)], axis=-1)
    return mix, mix_c


def conv_ffn(h, w_up, conv_w, conv_b, w_down):
    u = dwconv_centred(h @ w_up, conv_w, conv_b)
    gate, val = jnp.split(u, 2, axis=-1)
    return (jax.nn.silu(gate) * val) @ w_down


def setup_inputs(seed: int = 0) -> dict:
    key = jax.random.key(seed)
    ks = jax.random.split(key, 26)
    f32 = jnp.float32
    D = D_MODEL
    F = FFN_DIM

    def nrm(k, shape, s):
        return jax.random.normal(k, shape, f32) * s

    ret_base = jnp.log(2.0 ** (5.0 + jnp.arange(N_HEADS, dtype=f32)) - 1.0)
    dt0 = jnp.exp(jax.random.uniform(ks[16], (DEPTH, 2, SSD_HEADS), f32, math.log(1e-3), math.log(1e-1)))
    return {
        'x': nrm(ks[0], (BATCH, SEQ, D), 1.0),
        'c': nrm(ks[1], (BATCH, D), 1.0),
        'ctx': nrm(ks[2], (BATCH, CTX_LEN, D), 1.0),
        'c_ctx': nrm(ks[3], (D,), 1.0),
        'w_ada': nrm(ks[4], (DEPTH, D, 6 * D), 0.5 * D ** -0.5),
        'b_ada': nrm(ks[5], (DEPTH, 6 * D), 0.02),
        'norm1_g': 1.0 + nrm(ks[6], (DEPTH, D), 0.05),
        'w_in': nrm(ks[7], (DEPTH, D, IN_COLS), D ** -0.5),
        'attn_qn_g': 1.0 + nrm(ks[8], (DEPTH, HEAD_DIM), 0.05),
        'attn_kn_g': 1.0 + nrm(ks[9], (DEPTH, HEAD_DIM), 0.05),
        'win_sink': nrm(ks[10], (DEPTH, N_HEADS), 0.5),
        'ret_decay_logit': ret_base + nrm(ks[11], (DEPTH, 2, N_HEADS), 0.1),
        'ret_gn_g': 1.0 + nrm(ks[12], (DEPTH, GROUP_WIDTH), 0.05),
        'ssd_conv_w': nrm(ks[13], (DEPTH, SSD_CONV, SSD_CONV_CH), SSD_CONV ** -0.5),
        'ssd_conv_b': nrm(ks[14], (DEPTH, SSD_CONV_CH), 0.02),
        'ssd_a_log': jnp.log(jax.random.uniform(ks[15], (DEPTH, 2, SSD_HEADS), f32, 1.0, 16.0)),
        'ssd_dt_bias': dt0 + jnp.log(-jnp.expm1(-dt0)),
        'ssd_d': 1.0 + nrm(ks[17], (DEPTH, SSD_HEADS), 0.1),
        'ssd_norm_g': 1.0 + nrm(ks[18], (DEPTH, GROUP_WIDTH), 0.05),
        'w_out': nrm(ks[19], (DEPTH, MIX_WIDTH, D), MIX_WIDTH ** -0.5),
        'norm2_g': 1.0 + nrm(ks[20], (DEPTH, D), 0.05),
        'ffn_w_up': nrm(ks[21], (DEPTH, D, 2 * F), D ** -0.5),
        'ffn_conv_w': nrm(ks[22], (DEPTH, FFN_CONV, 2 * F), FFN_CONV ** -0.5),
        'ffn_conv_b': nrm(ks[23], (DEPTH, 2 * F), 0.02),
        'ffn_w_down': nrm(ks[24], (DEPTH, F, D), F ** -0.5),
        'final_g': 1.0 + nrm(ks[25], (D,), 0.05),
    }


def reference(x, c, ctx, c_ctx, w_ada, b_ada, norm1_g, w_in, attn_qn_g, attn_kn_g, win_sink, ret_decay_logit,
              ret_gn_g, ssd_conv_w, ssd_conv_b, ssd_a_log, ssd_dt_bias, ssd_d, ssd_norm_g, w_out, norm2_g,
              ffn_w_up, ffn_conv_w, ffn_conv_b, ffn_w_down, final_g):
    b, l, d = x.shape
    rows = l // GRID_W
    row = jnp.repeat(jnp.arange(rows), GRID_W).astype(jnp.float32)
    col = jnp.tile(jnp.arange(GRID_W), rows).astype(jnp.float32)
    n_ax = HEAD_DIM // 4
    inv_ax = ROPE_THETA ** (-jnp.arange(n_ax, dtype=jnp.float32) / n_ax)
    ang_row = row[:, None] * inv_ax
    ang_col = col[:, None] * inv_ax
    n_ret = HEAD_DIM // 2
    inv_ret = RET_THETA ** (-jnp.arange(n_ret, dtype=jnp.float32) / n_ret)
    ang_ret = jnp.arange(l, dtype=jnp.float32)[:, None] * inv_ret

    xc = ctx
    for i in range(DEPTH):
        need_ctx = i < DEPTH - 1
        mod = (jax.nn.silu(c) @ w_ada[i] + b_ada[i]).reshape(b, 6, 1, d)
        mod_c = (jax.nn.silu(c_ctx) @ w_ada[i] + b_ada[i]).reshape(6, 1, d)
        sh1, sc1, g1, sh2, sc2, g2 = [mod[:, j] for j in range(6)]
        sh1c, sc1c, g1c, sh2c, sc2c, g2c = [mod_c[j] for j in range(6)]

        h = rmsnorm(x, norm1_g[i]) * (1.0 + sc1) + sh1
        hc = rmsnorm(xc, norm1_g[i]) * (1.0 + sc1c) + sh1c
        mix, mix_c = token_mixers(h @ w_in[i], hc @ w_in[i], ang_row, ang_col, ang_ret,
                                  attn_qn_g[i], attn_kn_g[i], win_sink[i], ret_decay_logit[i], ret_gn_g[i],
                                  ssd_conv_w[i], ssd_conv_b[i], ssd_a_log[i], ssd_dt_bias[i], ssd_d[i],
                                  ssd_norm_g[i], need_ctx)
        x = x + g1 * (mix @ w_out[i])
        h2 = rmsnorm(x, norm2_g[i]) * (1.0 + sc2) + sh2
        x = x + g2 * conv_ffn(h2, ffn_w_up[i], ffn_conv_w[i], ffn_conv_b[i], ffn_w_down[i])
        if need_ctx:
            xc = xc + g1c * (mix_c @ w_out[i])
            h2c = rmsnorm(xc, norm2_g[i]) * (1.0 + sc2c) + sh2c
            xc = xc + g2c * conv_ffn(h2c, ffn_w_up[i], ffn_conv_w[i], ffn_conv_b[i], ffn_w_down[i])
    return rmsnorm(x, final_g)
```

```python
import functools
import math

import jax
import jax.numpy as jnp
from jax import lax
from jax.experimental import pallas as pl
from jax.experimental.pallas import tpu as pltpu

D_MODEL = 2048
GRID_W = 64
HEAD_DIM = 64
GROUP_WIDTH = D_MODEL // 4
N_HEADS = GROUP_WIDTH // HEAD_DIM
KV_HEADS = 2
Q_PER_KV = N_HEADS // KV_HEADS
KV_WIDTH = KV_HEADS * HEAD_DIM
WINDOW = 128
CHUNK = 128
SSD_HEADS = GROUP_WIDTH // 64
SSD_STATE = 128
SSD_GROUPS = 2
SSD_BC = SSD_GROUPS * SSD_STATE
SSD_CONV = 5
SSD_CONV_CH = GROUP_WIDTH + 2 * SSD_BC
FFN_DIM = ((8 * D_MODEL // 3 + 255) // 256) * 256
FFN_CONV = 3
ROPE_THETA = 10000.0
RET_THETA = 10000.0
EPS = 1e-6
NEG_INF = -1e30
LOG2E = 1.4426950408889634
IN_COLS = 8 * GROUP_WIDTH + 4 * KV_WIDTH + 2 * SSD_BC + SSD_HEADS

IN_TILE = 768
IN_COLS_PAD = 7 * IN_TILE
COL_RET = 3
COL_SZ = 7
COL_XBC = 4
COL_DT = 40

LANES = 128
HALO = 16
MXU_DTYPE = jnp.bfloat16
VMEM_LIMIT = 56 * 2 ** 20

F32 = jnp.float32


def _cparams(*sem):
    return pltpu.CompilerParams(dimension_semantics=sem, vmem_limit_bytes=VMEM_LIMIT)


def _mm(a, b):
    return jnp.dot(a.astype(MXU_DTYPE), b.astype(MXU_DTYPE), preferred_element_type=F32)


def _mm_nt(a, b):
    return lax.dot_general(a.astype(MXU_DTYPE), b.astype(MXU_DTYPE), (((1,), (1,)), ((), ())),
                           preferred_element_type=F32)


def _silu(x):
    return x * jax.nn.sigmoid(x)


def _softplus(x):
    return jnp.maximum(x, 0.0) + jnp.log1p(jnp.exp(-jnp.abs(x)))


def _log_sigmoid(x):
    return jnp.minimum(x, 0.0) - jnp.log1p(jnp.exp(-jnp.abs(x)))


def _rms(x):
    return x * lax.rsqrt(jnp.mean(x * x, axis=-1, keepdims=True) + EPS)


def _ada_kernel(c_ref, w_ref, b_ref, o_ref):
    o_ref[0] = jnp.dot(_silu(c_ref[...]), w_ref[0], preferred_element_type=F32) + b_ref[0]


def _ada(cvec, w_ada, b_ada):
    depth, d, n = w_ada.shape
    tn = 1024
    return pl.pallas_call(
        _ada_kernel,
        grid=(depth, n // tn),
        in_specs=[pl.BlockSpec((8, d), lambda l, j: (0, 0)),
                  pl.BlockSpec((1, d, tn), lambda l, j: (l, 0, j)),
                  pl.BlockSpec((1, 1, tn), lambda l, j: (l, 0, j))],
        out_specs=pl.BlockSpec((1, 8, tn), lambda l, j: (l, 0, j)),
        out_shape=jax.ShapeDtypeStruct((depth, 8, n), F32),
        compiler_params=_cparams("parallel", "parallel"),
        name="ada",
    )(cvec, w_ada, b_ada.reshape(depth, 1, n))


def _inproj_kernel(x_ref, g_ref, sc_ref, sh_ref, w_ref, o_ref, h_sc):
    @pl.when(pl.program_id(2) == 0)
    def _():
        h = _rms(x_ref[0]) * g_ref[...] * (1.0 + sc_ref[0]) + sh_ref[0]
        h_sc[...] = h.astype(h_sc.dtype)

    o_ref[0] = jnp.dot(h_sc[...], w_ref[...], preferred_element_type=F32)


def _in_proj(x, g, sc, sh, w):
    nb, lx, d = x.shape
    tm = min(512, lx)
    return pl.pallas_call(
        _inproj_kernel,
        grid=(nb, lx // tm, IN_COLS_PAD // IN_TILE),
        in_specs=[pl.BlockSpec((1, tm, d), lambda b, i, j: (b, i, 0)),
                  pl.BlockSpec((1, d), lambda b, i, j: (0, 0)),
                  pl.BlockSpec((1, 1, d), lambda b, i, j: (b, 0, 0)),
                  pl.BlockSpec((1, 1, d), lambda b, i, j: (b, 0, 0)),
                  pl.BlockSpec((d, IN_TILE), lambda b, i, j: (0, j))],
        out_specs=pl.BlockSpec((1, tm, IN_TILE), lambda b, i, j: (b, i, j)),
        out_shape=jax.ShapeDtypeStruct((nb, lx, IN_COLS_PAD), F32),
        scratch_shapes=[pltpu.VMEM((tm, d), MXU_DTYPE)],
        compiler_params=_cparams("parallel", "parallel", "arbitrary"),
        name="in_proj",
    )(x, g, sc, sh, w)


def _rope_apply(x, cos, sa, sb, shift):
    return x * cos + pltpu.roll(x, LANES - shift, 1) * sa + pltpu.roll(x, shift, 1) * sb


def _qkv_prep_kernel(p_ref, cos_ref, sa_ref, sb_ref, gq_ref, gk_ref, bd_ref, q_ref, kt_ref, v_ref, *,
                     norm, rope, qscale):
    x = p_ref[0]

    def prep(t, g):
        if norm:
            sq = t * t
            hi = sq.astype(jnp.bfloat16)
            lo = (sq - hi.astype(F32)).astype(jnp.bfloat16)
            ss = (jnp.dot(hi, bd_ref[...], preferred_element_type=F32)
                  + jnp.dot(lo, bd_ref[...], preferred_element_type=F32))
            t = t * lax.rsqrt(ss * (1.0 / HEAD_DIM) + EPS) * g
        if rope:
            t = _rope_apply(t, cos_ref[...], sa_ref[...], sb_ref[...], HEAD_DIM // 4)
        return t

    for j in range(N_HEADS // 2):
        r = prep(x[:, j * LANES:(j + 1) * LANES], gq_ref[...]) * qscale
        q_ref[0, 2 * j] = r[:, :HEAD_DIM].astype(q_ref.dtype)
        q_ref[0, 2 * j + 1] = r[:, HEAD_DIM:].astype(q_ref.dtype)
    kt = prep(x[:, GROUP_WIDTH:GROUP_WIDTH + KV_WIDTH], gk_ref[...]).T
    kt_ref[0, 0] = kt[:HEAD_DIM].astype(kt_ref.dtype)
    kt_ref[0, 1] = kt[HEAD_DIM:].astype(kt_ref.dtype)
    v = x[:, GROUP_WIDTH + KV_WIDTH:]
    v_ref[0, 0] = v[:, :HEAD_DIM].astype(v_ref.dtype)
    v_ref[0, 1] = v[:, HEAD_DIM:].astype(v_ref.dtype)


def _qkv_prep(p, col, tabs, gq, gk, bd, *, norm, rope, qscale):
    nb, lx, _ = p.shape
    tl = min(256, lx)
    cos, sa, sb = tabs
    kern = functools.partial(_qkv_prep_kernel, norm=norm, rope=rope, qscale=qscale)
    tab_spec = pl.BlockSpec((tl, LANES), lambda b, i: (i, 0))
    row_spec = pl.BlockSpec((1, LANES), lambda b, i: (0, 0))
    return pl.pallas_call(
        kern,
        grid=(nb, lx // tl),
        in_specs=[pl.BlockSpec((1, tl, IN_TILE), lambda b, i: (b, i, col)),
                  tab_spec, tab_spec, tab_spec, row_spec, row_spec,
                  pl.BlockSpec((LANES, LANES), lambda b, i: (0, 0))],
        out_specs=[pl.BlockSpec((1, N_HEADS, tl, HEAD_DIM), lambda b, i: (b, 0, i, 0)),
                   pl.BlockSpec((1, KV_HEADS, HEAD_DIM, tl), lambda b, i: (b, 0, 0, i)),
                   pl.BlockSpec((1, KV_HEADS, tl, HEAD_DIM), lambda b, i: (b, 0, i, 0))],
        out_shape=[jax.ShapeDtypeStruct((nb, N_HEADS, lx, HEAD_DIM), MXU_DTYPE),
                   jax.ShapeDtypeStruct((nb, KV_HEADS, HEAD_DIM, lx), MXU_DTYPE),
                   jax.ShapeDtypeStruct((nb, KV_HEADS, lx, HEAD_DIM), MXU_DTYPE)],
        compiler_params=_cparams("parallel", "parallel"),
        name="qkv_prep",
    )(p, cos, sa, sb, gq, gk, bd)


def _flash_kernel(*refs, has_sink, tq):
    if has_sink:
        q_ref, kt_ref, v_ref, sink_ref, o_ref, m_sc, l_sc, acc_sc = refs
    else:
        q_ref, kt_ref, v_ref, o_ref, m_sc, l_sc, acc_sc = refs
    ki = pl.program_id(3)
    rows = Q_PER_KV * tq

    @pl.when(ki == 0)
    def _():
        if has_sink:
            m_sc[...] = sink_ref[0] * LOG2E
            l_sc[...] = jnp.ones_like(l_sc)
        else:
            m_sc[...] = jnp.full_like(m_sc, -jnp.inf)
            l_sc[...] = jnp.zeros_like(l_sc)
        acc_sc[...] = jnp.zeros_like(acc_sc)

    q = q_ref[0].reshape(rows, HEAD_DIM)
    kt = kt_ref[0, 0]
    s = jnp.dot(q, kt, preferred_element_type=F32)
    m_prev = m_sc[...]
    m_new = jnp.maximum(m_prev, jnp.max(s, axis=1, keepdims=True))
    alpha = jnp.exp2(m_prev - m_new)
    p = jnp.exp2(s - m_new[:, :1])
    l_sc[...] = alpha * l_sc[...] + jnp.sum(p, axis=1, keepdims=True)
    acc_sc[...] = acc_sc[...] * alpha[:, :HEAD_DIM] + jnp.dot(p.astype(v_ref.dtype), v_ref[0, 0],
                                                                preferred_element_type=F32)
    m_sc[...] = m_new

    @pl.when(ki == pl.num_programs(3) - 1)
    def _():
        o = acc_sc[...] / l_sc[...][:, :HEAD_DIM]
        for h in range(Q_PER_KV):
            o_ref[0, :, h * HEAD_DIM:(h + 1) * HEAD_DIM] = o[h * tq:(h + 1) * tq].astype(o_ref.dtype)


def _flash(q, kt, v, sink_rows=None, *, tq, tk):
    nb, _, lq, _ = q.shape
    lk = kt.shape[-1]
    rows = Q_PER_KV * tq
    has_sink = sink_rows is not None
    in_specs = [pl.BlockSpec((1, Q_PER_KV, tq, HEAD_DIM), lambda b, k, i, j: (b, k, i, 0)),
                pl.BlockSpec((1, 1, HEAD_DIM, tk), lambda b, k, i, j: (b, k, 0, j)),
                pl.BlockSpec((1, 1, tk, HEAD_DIM), lambda b, k, i, j: (b, k, j, 0))]
    args = [q, kt, v]
    if has_sink:
        in_specs.append(pl.BlockSpec((1, rows, LANES), lambda b, k, i, j: (k, 0, 0)))
        args.append(sink_rows)
    return pl.pallas_call(
        functools.partial(_flash_kernel, has_sink=has_sink, tq=tq),
        grid=(nb, KV_HEADS, lq // tq, lk // tk),
        in_specs=in_specs,
        out_specs=pl.BlockSpec((1, tq, Q_PER_KV * HEAD_DIM), lambda b, k, i, j: (b, i, k)),
        out_shape=jax.ShapeDtypeStruct((nb, lq, GROUP_WIDTH), MXU_DTYPE),
        scratch_shapes=[pltpu.VMEM((rows, LANES), F32), pltpu.VMEM((rows, LANES), F32),
                        pltpu.VMEM((rows, HEAD_DIM), F32)],
        compiler_params=_cparams("parallel", "parallel", "parallel", "arbitrary"),
        name="flash_attn",
    )(*args)


def _window_kernel(q_ref, ktp_ref, ktc_ref, ktn_ref, ktx_ref, vp_ref, vc_ref, vn_ref, vx_ref, sink_ref, o_ref):
    i = pl.program_id(2)
    nblk = pl.num_programs(2)
    rows = Q_PER_KV * WINDOW
    q = q_ref[0].reshape(rows, HEAD_DIM)
    sp = jnp.dot(q, ktp_ref[0, 0], preferred_element_type=F32)
    sc = jnp.dot(q, ktc_ref[0, 0], preferred_element_type=F32)
    sn = jnp.dot(q, ktn_ref[0, 0], preferred_element_type=F32)
    sx = jnp.dot(q, ktx_ref[0, 0], preferred_element_type=F32)
    qpos = lax.broadcasted_iota(jnp.int32, (rows, WINDOW), 0) % WINDOW
    kpos = lax.broadcasted_iota(jnp.int32, (rows, WINDOW), 1)
    rel = kpos - qpos
    lo = jnp.where(i > 0, 0, 2 * WINDOW)
    hi = jnp.where(i < nblk - 1, 0, -2 * WINDOW)
    sp = jnp.where(rel >= lo, sp, NEG_INF)
    sn = jnp.where(rel <= hi, sn, NEG_INF)
    sink = sink_ref[0] * LOG2E
    m = jnp.maximum(jnp.maximum(jnp.max(sp, axis=1, keepdims=True), jnp.max(sc, axis=1, keepdims=True)),
                    jnp.maximum(jnp.max(sn, axis=1, keepdims=True), jnp.max(sx, axis=1, keepdims=True)))
    m = jnp.maximum(m, sink)
    m1 = m[:, :1]
    pp, pc, pn, px = jnp.exp2(sp - m1), jnp.exp2(sc - m1), jnp.exp2(sn - m1), jnp.exp2(sx - m1)
    den = (jnp.sum(pp, axis=1, keepdims=True) + jnp.sum(pc, axis=1, keepdims=True)
           + jnp.sum(pn, axis=1, keepdims=True) + jnp.sum(px, axis=1, keepdims=True) + jnp.exp2(sink - m))
    dt = vp_ref.dtype
    o = (jnp.dot(pp.astype(dt), vp_ref[0, 0], preferred_element_type=F32)
         + jnp.dot(pc.astype(dt), vc_ref[0, 0], preferred_element_type=F32)
         + jnp.dot(pn.astype(dt), vn_ref[0, 0], preferred_element_type=F32)
         + jnp.dot(px.astype(dt), vx_ref[0, 0], preferred_element_type=F32))
    o = o / den[:, :HEAD_DIM]
    for h in range(Q_PER_KV):
        o_ref[0, :, h * HEAD_DIM:(h + 1) * HEAD_DIM] = o[h * WINDOW:(h + 1) * WINDOW].astype(o_ref.dtype)


def _window(q, kt, v, ktx, vx, sink_rows):
    nb, _, lq, _ = q.shape
    lc = ktx.shape[-1]
    nblk = lq // WINDOW
    rows = Q_PER_KV * WINDOW

    def kspec(off):
        return pl.BlockSpec((1, 1, HEAD_DIM, WINDOW),
                            lambda b, k, i: (b, k, 0, jnp.clip(i + off, 0, nblk - 1)))

    def vspec(off):
        return pl.BlockSpec((1, 1, WINDOW, HEAD_DIM),
                            lambda b, k, i: (b, k, jnp.clip(i + off, 0, nblk - 1), 0))

    return pl.pallas_call(
        _window_kernel,
        grid=(nb, KV_HEADS, nblk),
        in_specs=[pl.BlockSpec((1, Q_PER_KV, WINDOW, HEAD_DIM), lambda b, k, i: (b, k, i, 0)),
                  kspec(-1), kspec(0), kspec(1),
                  pl.BlockSpec((1, 1, HEAD_DIM, lc), lambda b, k, i: (b, k, 0, 0)),
                  vspec(-1), vspec(0), vspec(1),
                  pl.BlockSpec((1, 1, lc, HEAD_DIM), lambda b, k, i: (b, k, 0, 0)),
                  pl.BlockSpec((1, rows, LANES), lambda b, k, i: (k, 0, 0))],
        out_specs=pl.BlockSpec((1, WINDOW, Q_PER_KV * HEAD_DIM), lambda b, k, i: (b, i, k)),
        out_shape=jax.ShapeDtypeStruct((nb, lq, GROUP_WIDTH), MXU_DTYPE),
        compiler_params=_cparams("parallel", "parallel", "parallel"),
        name="window_attn",
    )(q, kt, kt, kt, ktx, v, v, v, vx, sink_rows)


def _retention_kernel(*refs, reverse, rope, final):
    refs = list(refs)
    q_ref, k_ref, v_ref = refs[:3]
    del refs[:3]
    if rope:
        cos_ref, sa_ref, sb_ref = refs[:3]
        del refs[:3]
    lg_ref, lgw_ref, s0_ref = refs[:3]
    del refs[:3]
    if final:
        of_ref, gate_ref, gn_ref = refs[:3]
        del refs[:3]
    o_ref, sfin_ref, s_sc, d_sc = refs
    c = CHUNK
    n = pl.program_id(1)

    @pl.when(n == 0)
    def _():
        s_sc[...] = s0_ref[0]
        ri = lax.broadcasted_iota(jnp.int32, (c, c), 0)
        ci = lax.broadcasted_iota(jnp.int32, (c, c), 1)
        rel = (ci - ri if reverse else ri - ci).astype(F32)
        for h in range(N_HEADS):
            ld = _log_sigmoid(lg_ref[h])
            d_sc[h] = jnp.where(rel >= 0, jnp.exp(jnp.maximum(rel, 0.0) * ld), 0.0)

    q, k, v = q_ref[0], k_ref[0], v_ref[0]
    if rope:
        cos, sa, sb = cos_ref[...], sa_ref[...], sb_ref[...]
        q = jnp.concatenate([_rope_apply(q[:, j * LANES:(j + 1) * LANES], cos, sa, sb, HEAD_DIM // 2)
                             for j in range(GROUP_WIDTH // LANES)], axis=1)
        k = jnp.concatenate([_rope_apply(k[:, j * LANES:(j + 1) * LANES], cos, sa, sb, HEAD_DIM // 2)
                             for j in range(GROUP_WIDTH // LANES)], axis=1)
    k = k * (HEAD_DIM ** -0.5)
    ldw = _log_sigmoid(lgw_ref[...])
    idx = lax.broadcasted_iota(jnp.int32, (c, GROUP_WIDTH), 0).astype(F32)
    if reverse:
        q_in = q * jnp.exp((c - idx) * ldw)
        k_out = k * jnp.exp(idx * ldw)
    else:
        q_in = q * jnp.exp((idx + 1.0) * ldw)
        k_out = k * jnp.exp((c - 1.0 - idx) * ldw)

    outs = []
    for h in range(N_HEADS):
        sl = slice(h * HEAD_DIM, (h + 1) * HEAD_DIM)
        s = _mm_nt(q[:, sl], k[:, sl]) * d_sc[h]
        o = _mm(s, v[:, sl]) + _mm(q_in[:, sl], s_sc[h])
        kv = lax.dot_general(k_out[:, sl].astype(MXU_DTYPE), v[:, sl].astype(MXU_DTYPE),
                             (((0,), (0,)), ((), ())), preferred_element_type=F32)
        chunk_decay = jnp.exp(c * _log_sigmoid(lg_ref[h]))[:, :HEAD_DIM]
        s_sc[h] = s_sc[h] * chunk_decay + kv
        if final:
            o = o + of_ref[0][:, sl]
            mu = jnp.mean(o, axis=1, keepdims=True)
            var = jnp.mean(jnp.square(o - mu), axis=1, keepdims=True)
            o = (o - mu) * lax.rsqrt(var + EPS)
        outs.append(o)
    o = jnp.concatenate(outs, axis=1)
    if final:
        o = o * gn_ref[...] * _silu(gate_ref[0])
    o_ref[0] = o.astype(o_ref.dtype)

    @pl.when(n == pl.num_programs(1) - 1)
    def _():
        sfin_ref[0] = s_sc[...]


def _retention(p, tabs, lg, lgw, s0, *, reverse, o_fwd=None, gn=None):
    nb, lx, _ = p.shape
    c = CHUNK
    nc = lx // c
    rope = tabs is not None
    final = o_fwd is not None

    def cm(n):
        return nc - 1 - n if reverse else n

    def col(j):
        return pl.BlockSpec((1, c, GROUP_WIDTH), lambda b, n: (b, cm(n), j))

    in_specs = [col(COL_RET), col(COL_RET + 1), col(COL_RET + 2)]
    args = [p, p, p]
    if rope:
        in_specs += [pl.BlockSpec((c, LANES), lambda b, n: (cm(n), 0))] * 3
        args += list(tabs)
    in_specs += [pl.BlockSpec((N_HEADS, 1, LANES), lambda b, n: (0, 0, 0)),
                 pl.BlockSpec((1, GROUP_WIDTH), lambda b, n: (0, 0)),
                 pl.BlockSpec((1, N_HEADS, HEAD_DIM, HEAD_DIM), lambda b, n: (b, 0, 0, 0))]
    args += [lg, lgw, s0]
    if final:
        in_specs += [pl.BlockSpec((1, c, GROUP_WIDTH), lambda b, n: (b, cm(n), 0)), col(COL_RET + 3),
                     pl.BlockSpec((1, GROUP_WIDTH), lambda b, n: (0, 0))]
        args += [o_fwd, p, gn]
    return pl.pallas_call(
        functools.partial(_retention_kernel, reverse=reverse, rope=rope, final=final),
        grid=(nb, nc),
        in_specs=in_specs,
        out_specs=[pl.BlockSpec((1, c, GROUP_WIDTH), lambda b, n: (b, cm(n), 0)),
                   pl.BlockSpec((1, N_HEADS, HEAD_DIM, HEAD_DIM), lambda b, n: (b, 0, 0, 0))],
        out_shape=[jax.ShapeDtypeStruct((nb, lx, GROUP_WIDTH), MXU_DTYPE if final else F32),
                   jax.ShapeDtypeStruct((nb, N_HEADS, HEAD_DIM, HEAD_DIM), F32)],
        scratch_shapes=[pltpu.VMEM((N_HEADS, HEAD_DIM, HEAD_DIM), F32), pltpu.VMEM((N_HEADS, c, c), F32)],
        compiler_params=_cparams("parallel", "arbitrary"),
        name="retention",
    )(*args)


def _ssd_conv_kernel(xp_ref, xc_ref, xn_ref, w_ref, b_ref, o_ref, x_sc):
    i = pl.program_id(1)
    tl = xc_ref.shape[1]
    x_sc[0:HALO] = jnp.where(i > 0, xp_ref[0], 0.0)
    x_sc[HALO:HALO + tl] = xc_ref[0]
    x_sc[HALO + tl:] = jnp.where(i < pl.num_programs(1) - 1, xn_ref[0], 0.0)
    acc = jnp.zeros((tl, xc_ref.shape[2]), F32) + b_ref[...]
    for k in range(SSD_CONV):
        off = HALO + k - SSD_CONV // 2
        acc = acc + x_sc[off:off + tl] * w_ref[k:k + 1]
    o_ref[0] = _silu(acc)


def _ssd_conv(p, w, b):
    nb, lx, _ = p.shape
    tl = min(256, lx)
    nh = lx // HALO
    r = tl // HALO
    ch = SSD_CONV_CH
    col = COL_XBC
    return pl.pallas_call(
        _ssd_conv_kernel,
        grid=(nb, lx // tl),
        in_specs=[pl.BlockSpec((1, HALO, ch), lambda b_, i: (b_, jnp.maximum(i * r - 1, 0), col)),
                  pl.BlockSpec((1, tl, ch), lambda b_, i: (b_, i, col)),
                  pl.BlockSpec((1, HALO, ch), lambda b_, i: (b_, jnp.minimum((i + 1) * r, nh - 1), col)),
                  pl.BlockSpec((SSD_CONV, ch), lambda b_, i: (0, 0)),
                  pl.BlockSpec((1, ch), lambda b_, i: (0, 0))],
        out_specs=pl.BlockSpec((1, tl, ch), lambda b_, i: (b_, i, 0)),
        out_shape=jax.ShapeDtypeStruct((nb, lx, ch), F32),
        scratch_shapes=[pltpu.VMEM((tl + 2 * HALO, ch), F32)],
        compiler_params=_cparams("parallel", "parallel"),
        name="ssd_conv",
    )(p, p, p, w, b)


def _ssd_kernel(*refs, reverse, final):
    refs = list(refs)
    xbc_ref, dt_ref, dtt_ref, bias_row_ref, alog_row_ref, bias_col_ref, alog_col_ref, h0_ref = refs[:8]
    del refs[:8]
    if final:
        yf_ref, z_ref, dskip_ref, ng_ref = refs[:4]
        del refs[:4]
    y_ref, hfin_ref, h_sc = refs
    c = CHUNK
    n = pl.program_id(1)
    hp = lax.Precision.HIGHEST

    @pl.when(n == 0)
    def _():
        h_sc[...] = h0_ref[0]

    ri = lax.broadcasted_iota(jnp.int32, (c, c), 0)
    ci = lax.broadcasted_iota(jnp.int32, (c, c), 1)
    tri_b = (ri <= ci) if reverse else (ri >= ci)
    tri = tri_b.astype(F32)

    dt = _softplus(dt_ref[0] + bias_row_ref[...])
    a = dt * (-jnp.exp(alog_row_ref[...]))
    a_cum = jnp.dot(tri, a, precision=hp, preferred_element_type=F32)
    dtt = _softplus(dtt_ref[0] + bias_col_ref[...])
    at = dtt * (-jnp.exp(alog_col_ref[...]))
    a_cum_t = lax.dot_general(at, tri, (((1,), (1,)), ((), ())), precision=hp, preferred_element_type=F32)
    a_tot_t = jnp.dot(at, jnp.ones((c, c), F32), precision=hp, preferred_element_type=F32)

    xbc = xbc_ref[0]
    x = xbc[:, :GROUP_WIDTH]
    xt = x.T
    outs = []
    for g in range(SSD_GROUPS):
        bm = xbc[:, GROUP_WIDTH + g * SSD_STATE:GROUP_WIDTH + (g + 1) * SSD_STATE]
        cmat = xbc[:, GROUP_WIDTH + SSD_BC + g * SSD_STATE:GROUP_WIDTH + SSD_BC + (g + 1) * SSD_STATE]
        cb = _mm_nt(cmat, bm)
        for hh in range(SSD_HEADS // SSD_GROUPS):
            h = g * (SSD_HEADS // SSD_GROUPS) + hh
            sl = slice(h * 64, (h + 1) * 64)
            col = a_cum[:, h:h + 1]
            row = a_cum_t[h:h + 1, :]
            dt_row = dtt[h:h + 1, :]
            tot_row = a_tot_t[h:h + 1, :]
            decay = jnp.exp(jnp.where(tri_b, col - row, NEG_INF))
            w = cb * decay * dt_row
            y = _mm(w, x[:, sl]) + _mm_nt(cmat, h_sc[h]) * jnp.exp(col)
            to_end = jnp.exp(tot_row - row) * dt_row
            st = _mm(xt[sl, :] * to_end, bm)
            h_sc[h] = h_sc[h] * jnp.exp(tot_row) + st
            outs.append(y)
    y = jnp.concatenate(outs, axis=1)
    if final:
        y = (yf_ref[0] + y + dskip_ref[...] * x) * _silu(z_ref[0])
        half = GROUP_WIDTH // SSD_GROUPS
        y = jnp.concatenate([_rms(y[:, :half]), _rms(y[:, half:])], axis=1) * ng_ref[...]
    y_ref[0] = y.astype(y_ref.dtype)

    @pl.when(n == pl.num_programs(1) - 1)
    def _():
        hfin_ref[0] = h_sc[...]


def _ssd(xbc, p, dtt, prm, h0, *, reverse, y_fwd=None, dskip=None, ng=None):
    nb, lx, _ = xbc.shape
    c = CHUNK
    nc = lx // c
    final = y_fwd is not None

    def cm(n):
        return nc - 1 - n if reverse else n

    row = pl.BlockSpec((1, LANES), lambda b, n: (0, 0))
    colb = pl.BlockSpec((SSD_HEADS, LANES), lambda b, n: (0, 0))
    wide = pl.BlockSpec((1, GROUP_WIDTH), lambda b, n: (0, 0))
    state = pl.BlockSpec((1, SSD_HEADS, 64, SSD_STATE), lambda b, n: (b, 0, 0, 0))
    in_specs = [pl.BlockSpec((1, c, SSD_CONV_CH), lambda b, n: (b, cm(n), 0)),
                pl.BlockSpec((1, c, LANES), lambda b, n: (b, cm(n), COL_DT)),
                pl.BlockSpec((1, SSD_HEADS, c), lambda b, n: (b, 0, cm(n))),
                row, row, colb, colb, state]
    args = [xbc, p, dtt, *prm, h0]
    if final:
        in_specs += [pl.BlockSpec((1, c, GROUP_WIDTH), lambda b, n: (b, cm(n), 0)),
                     pl.BlockSpec((1, c, GROUP_WIDTH), lambda b, n: (b, cm(n), COL_SZ)), wide, wide]
        args += [y_fwd, p, dskip, ng]
    return pl.pallas_call(
        functools.partial(_ssd_kernel, reverse=reverse, final=final),
        grid=(nb, nc),
        in_specs=in_specs,
        out_specs=[pl.BlockSpec((1, c, GROUP_WIDTH), lambda b, n: (b, cm(n), 0)), state],
        out_shape=[jax.ShapeDtypeStruct((nb, lx, GROUP_WIDTH), MXU_DTYPE if final else F32),
                   jax.ShapeDtypeStruct((nb, SSD_HEADS, 64, SSD_STATE), F32)],
        scratch_shapes=[pltpu.VMEM((SSD_HEADS, 64, SSD_STATE), F32)],
        compiler_params=_cparams("parallel", "arbitrary"),
        name="ssd_scan",
    )(*args)


def _outproj_kernel(a_ref, w_ref_in, r_ref, s_ref, x_ref, g_ref, w_ref, o_ref):
    acc = jnp.zeros(o_ref.shape[1:], F32)
    for m, ref in enumerate((a_ref, w_ref_in, r_ref, s_ref)):
        acc = acc + jnp.dot(ref[0], w_ref[m * GROUP_WIDTH:(m + 1) * GROUP_WIDTH, :], preferred_element_type=F32)
    o_ref[0] = x_ref[0] + g_ref[0] * acc


def _out_proj(mix, x, g1, w):
    nb, lx, d = x.shape
    tm = min(512, lx)
    mspec = pl.BlockSpec((1, tm, GROUP_WIDTH), lambda b, i: (b, i, 0))
    return pl.pallas_call(
        _outproj_kernel,
        grid=(nb, lx // tm),
        in_specs=[mspec, mspec, mspec, mspec,
                  pl.BlockSpec((1, tm, d), lambda b, i: (b, i, 0)),
                  pl.BlockSpec((1, 1, d), lambda b, i: (b, 0, 0)),
                  pl.BlockSpec((4 * GROUP_WIDTH, d), lambda b, i: (0, 0))],
        out_specs=pl.BlockSpec((1, tm, d), lambda b, i: (b, i, 0)),
        out_shape=jax.ShapeDtypeStruct((nb, lx, d), F32),
        compiler_params=_cparams("parallel", "parallel"),
        name="out_proj",
    )(*mix, x, g1, w)


def _ffn_kernel(*refs, final):
    refs = list(refs)
    (xp_ref, xc_ref, xn_ref, g_ref, sc_ref, sh_ref, gate_ref, wg_ref, wv_ref, cwg_ref, cwv_ref, cbg_ref, cbv_ref,
     wd_ref) = refs[:14]
    del refs[:14]
    if final:
        fg_ref = refs.pop(0)
    o_ref, h_sc, acc_sc, ug_sc, uv_sc = refs
    i, j = pl.program_id(1), pl.program_id(2)
    tm = xc_ref.shape[1]

    @pl.when(j == 0)
    def _():
        def nm(t):
            return (_rms(t) * g_ref[...] * (1.0 + sc_ref[0]) + sh_ref[0]).astype(h_sc.dtype)
        h_sc[0:HALO] = nm(xp_ref[0])
        h_sc[HALO:HALO + tm] = nm(xc_ref[0])
        h_sc[HALO + tm:] = nm(xn_ref[0])
        acc_sc[...] = jnp.zeros_like(acc_sc)

    h = h_sc[...]
    ug_sc[...] = jnp.dot(h, wg_ref[...], preferred_element_type=F32)
    uv_sc[...] = jnp.dot(h, wv_ref[...], preferred_element_type=F32)

    @pl.when(i == 0)
    def _():
        ug_sc[0:HALO] = jnp.zeros((HALO, ug_sc.shape[1]), F32)
        uv_sc[0:HALO] = jnp.zeros((HALO, uv_sc.shape[1]), F32)

    @pl.when(i == pl.num_programs(1) - 1)
    def _():
        ug_sc[HALO + tm:] = jnp.zeros((HALO, ug_sc.shape[1]), F32)
        uv_sc[HALO + tm:] = jnp.zeros((HALO, uv_sc.shape[1]), F32)

    def conv(u_sc, w_ref, b_ref):
        acc = jnp.zeros((tm, u_sc.shape[1]), F32) + b_ref[...]
        for k in range(FFN_CONV):
            off = HALO + k - FFN_CONV // 2
            acc = acc + u_sc[off:off + tm] * w_ref[k:k + 1]
        return acc

    act = _silu(conv(ug_sc, cwg_ref, cbg_ref)) * conv(uv_sc, cwv_ref, cbv_ref)
    acc_sc[...] += jnp.dot(act.astype(wd_ref.dtype), wd_ref[...], preferred_element_type=F32)

    @pl.when(j == pl.num_programs(2) - 1)
    def _():
        y = xc_ref[0] + gate_ref[0] * acc_sc[...]
        if final:
            y = _rms(y) * fg_ref[...]
        o_ref[0] = y


def _ffn(x, g, sc, sh, gate, w_up, conv_w, conv_b, w_down, final_g=None):
    nb, lx, d = x.shape
    f = w_down.shape[0]
    tm = min(512, lx)
    tf = 512
    nf = f // tf
    nh = lx // HALO
    r = tm // HALO
    final = final_g is not None
    vec = pl.BlockSpec((1, d), lambda b, i, j: (0, 0))
    mod = pl.BlockSpec((1, 1, d), lambda b, i, j: (b, 0, 0))
    in_specs = [pl.BlockSpec((1, HALO, d), lambda b, i, j: (b, jnp.maximum(i * r - 1, 0), 0)),
                pl.BlockSpec((1, tm, d), lambda b, i, j: (b, i, 0)),
                pl.BlockSpec((1, HALO, d), lambda b, i, j: (b, jnp.minimum((i + 1) * r, nh - 1), 0)),
                vec, mod, mod, mod,
                pl.BlockSpec((d, tf), lambda b, i, j: (0, j)),
                pl.BlockSpec((d, tf), lambda b, i, j: (0, nf + j)),
                pl.BlockSpec((FFN_CONV, tf), lambda b, i, j: (0, j)),
                pl.BlockSpec((FFN_CONV, tf), lambda b, i, j: (0, nf + j)),
                pl.BlockSpec((1, tf), lambda b, i, j: (0, j)),
                pl.BlockSpec((1, tf), lambda b, i, j: (0, nf + j)),
                pl.BlockSpec((tf, d), lambda b, i, j: (j, 0))]
    args = [x, x, x, g, sc, sh, gate, w_up, w_up, conv_w, conv_w, conv_b, conv_b, w_down]
    if final:
        in_specs.append(vec)
        args.append(final_g)
    return pl.pallas_call(
        functools.partial(_ffn_kernel, final=final),
        grid=(nb, lx // tm, nf),
        in_specs=in_specs,
        out_specs=pl.BlockSpec((1, tm, d), lambda b, i, j: (b, i, 0)),
        out_shape=jax.ShapeDtypeStruct((nb, lx, d), F32),
        scratch_shapes=[pltpu.VMEM((tm + 2 * HALO, d), MXU_DTYPE), pltpu.VMEM((tm, d), F32),
                        pltpu.VMEM((tm + 2 * HALO, tf), F32), pltpu.VMEM((tm + 2 * HALO, tf), F32)],
        compiler_params=_cparams("parallel", "parallel", "arbitrary"),
        name="conv_ffn",
    )(*args)


def _pair_tables(ang_blocks):
    cos, sa, sb = [], [], []
    for ang in ang_blocks:
        c, s = jnp.cos(ang), jnp.sin(ang)
        z = jnp.zeros_like(s)
        cos += [c, c]
        sa += [-s, z]
        sb += [z, s]
    def two(parts):
        t = jnp.concatenate(parts, axis=-1)
        return jnp.concatenate([t, t], axis=-1)
    return two(cos), two(sa), two(sb)


def _lane_rep(v, width):
    return jnp.repeat(v, width)[None, :]


def kernel(x, c, ctx, c_ctx, w_ada, b_ada, norm1_g, w_in, attn_qn_g, attn_kn_g, win_sink, ret_decay_logit,
           ret_gn_g, ssd_conv_w, ssd_conv_b, ssd_a_log, ssd_dt_bias, ssd_d, ssd_norm_g, w_out, norm2_g,
           ffn_w_up, ffn_conv_w, ffn_conv_b, ffn_w_down, final_g):
    b, l, d = x.shape
    lc = ctx.shape[1]
    depth = w_ada.shape[0]
    assert d == D_MODEL and l % 512 == 0 and lc % 256 == 0 and b + 1 <= 8

    rows = l // GRID_W
    row = jnp.repeat(jnp.arange(rows), GRID_W).astype(F32)
    colp = jnp.tile(jnp.arange(GRID_W), rows).astype(F32)
    n_ax = HEAD_DIM // 4
    inv_ax = ROPE_THETA ** (-jnp.arange(n_ax, dtype=F32) / n_ax)
    tabs_2d = _pair_tables([row[:, None] * inv_ax, colp[:, None] * inv_ax])
    n_ret = HEAD_DIM // 2
    inv_ret = RET_THETA ** (-jnp.arange(n_ret, dtype=F32) / n_ret)
    tabs_ret = _pair_tables([jnp.arange(l, dtype=F32)[:, None] * inv_ret])
    tabs_ctx = (jnp.ones((lc, LANES), F32), jnp.zeros((lc, LANES), F32), jnp.zeros((lc, LANES), F32))
    bd = (jnp.arange(LANES)[:, None] // HEAD_DIM == jnp.arange(LANES)[None, :] // HEAD_DIM).astype(jnp.bfloat16)

    cvec = jnp.zeros((8, d), F32).at[:b].set(c).at[b].set(c_ctx)
    mod_all = _ada(cvec, w_ada, b_ada)

    qscale = HEAD_DIM ** -0.5 * LOG2E
    xc = ctx
    for i in range(depth):
        need_ctx = i < depth - 1
        mod = mod_all[i].reshape(8, 6, 1, d)
        sh1, sc1, g1, sh2, sc2, g2 = [mod[:b, j] for j in range(6)]
        sh1c, sc1c, g1c, sh2c, sc2c, g2c = [jnp.broadcast_to(mod[b:b + 1, j], (b, 1, d)) for j in range(6)]

        w_in_p = jnp.pad(w_in[i], ((0, 0), (0, IN_COLS_PAD - IN_COLS))).astype(MXU_DTYPE)
        n1 = norm1_g[i][None, :]
        p = _in_proj(x, n1, sc1, sh1, w_in_p)
        pc = _in_proj(xc, n1, sc1c, sh1c, w_in_p)

        gq = jnp.tile(attn_qn_g[i], 2)[None, :]
        gk = jnp.tile(attn_kn_g[i], 2)[None, :]
        qa, kta, va = _qkv_prep(p, 0, tabs_2d, gq, gk, bd, norm=True, rope=True, qscale=qscale)
        qac, ktac, vac = _qkv_prep(pc, 0, tabs_ctx, gq, gk, bd, norm=True, rope=False, qscale=qscale)
        qw, ktw, vw = _qkv_prep(p, 1, tabs_2d, gq, gk, bd, norm=False, rope=True, qscale=qscale)
        qwc, ktwc, vwc = _qkv_prep(pc, 1, tabs_ctx, gq, gk, bd, norm=False, rope=False, qscale=qscale)
        sink = win_sink[i].reshape(KV_HEADS, Q_PER_KV)

        def sink_rows(t):
            return jnp.broadcast_to(sink[:, :, None, None], (KV_HEADS, Q_PER_KV, t, LANES)).reshape(
                KV_HEADS, Q_PER_KV * t, LANES)

        tk = (l + lc) // 13 if (l + lc) % (13 * 256) == 0 else 128
        o_att = _flash(qa, jnp.concatenate([kta, ktac], axis=-1), jnp.concatenate([va, vac], axis=2),
                       tq=256, tk=tk)
        o_win = _window(qw, ktw, vw, ktwc, vwc, sink_rows(WINDOW))

        lg = jnp.broadcast_to(ret_decay_logit[i][:, :, None, None], (2, N_HEADS, 1, LANES))
        lgw = [_lane_rep(ret_decay_logit[i][dr], HEAD_DIM) for dr in range(2)]
        s0 = jnp.zeros((b, N_HEADS, HEAD_DIM, HEAD_DIM), F32)
        gn = ret_gn_g[i][None, :]
        oc_f, sc_f = _retention(pc, None, lg[0], lgw[0], s0, reverse=False)
        o_ret_c, sc_b = _retention(pc, None, lg[1], lgw[1], s0, reverse=True, o_fwd=oc_f, gn=gn)
        o_f, _ = _retention(p, tabs_ret, lg[0], lgw[0], sc_f, reverse=False)
        o_ret, _ = _retention(p, tabs_ret, lg[1], lgw[1], sc_b, reverse=True, o_fwd=o_f, gn=gn)

        conv_b = ssd_conv_b[i][None, :]
        xbc = _ssd_conv(p, ssd_conv_w[i], conv_b)
        xbcc = _ssd_conv(pc, ssd_conv_w[i], conv_b)
        dtt = jnp.swapaxes(p[:, :, COL_DT * LANES:COL_DT * LANES + SSD_HEADS], 1, 2)
        dttc = jnp.swapaxes(pc[:, :, COL_DT * LANES:COL_DT * LANES + SSD_HEADS], 1, 2)

        def ssd_prm(dr):
            pad = jnp.zeros((LANES - SSD_HEADS,), F32)
            return (jnp.concatenate([ssd_dt_bias[i][dr], pad])[None, :],
                    jnp.concatenate([ssd_a_log[i][dr], pad])[None, :],
                    jnp.broadcast_to(ssd_dt_bias[i][dr][:, None], (SSD_HEADS, LANES)),
                    jnp.broadcast_to(ssd_a_log[i][dr][:, None], (SSD_HEADS, LANES)))

        h0 = jnp.zeros((b, SSD_HEADS, 64, SSD_STATE), F32)
        dskip = _lane_rep(ssd_d[i], 64)
        ng = ssd_norm_g[i][None, :]
        yc_f, hc_f = _ssd(xbcc, pc, dttc, ssd_prm(0), h0, reverse=False)
        o_ssd_c, hc_b = _ssd(xbcc, pc, dttc, ssd_prm(1), h0, reverse=True, y_fwd=yc_f, dskip=dskip, ng=ng)
        y_f, _ = _ssd(xbc, p, dtt, ssd_prm(0), hc_f, reverse=False)
        o_ssd, _ = _ssd(xbc, p, dtt, ssd_prm(1), hc_b, reverse=True, y_fwd=y_f, dskip=dskip, ng=ng)

        w_out_b = w_out[i].astype(MXU_DTYPE)
        w_up_b = ffn_w_up[i].astype(MXU_DTYPE)
        w_down_b = ffn_w_down[i].astype(MXU_DTYPE)
        n2 = norm2_g[i][None, :]
        cb2 = ffn_conv_b[i][None, :]
        x = _out_proj((o_att, o_win, o_ret, o_ssd), x, g1, w_out_b)
        x = _ffn(x, n2, sc2, sh2, g2, w_up_b, ffn_conv_w[i], cb2, w_down_b,
                 final_g=None if need_ctx else final_g[None, :])
        if need_ctx:
            o_att_c = _flash(qac, ktac, vac, tq=lc, tk=lc)
            o_win_c = _flash(qwc, ktwc, vwc, sink_rows(lc), tq=lc, tk=lc)
            xc = _out_proj((o_att_c, o_win_c, o_ret_c, o_ssd_c), xc, g1c, w_out_b)
            xc = _ffn(xc, n2, sc2c, sh2c, g2c, w_up_b, ffn_conv_w[i], cb2, w_down_b)
    return x
```

```python
import functools
import math

import jax
import jax.numpy as jnp
from jax import lax
from jax.experimental import pallas as pl
from jax.experimental.pallas import tpu as pltpu

D_MODEL = 2048
GRID_W = 64
HEAD_DIM = 64
GROUP_WIDTH = D_MODEL // 4
N_HEADS = GROUP_WIDTH // HEAD_DIM
KV_HEADS = 2
Q_PER_KV = N_HEADS // KV_HEADS
KV_WIDTH = KV_HEADS * HEAD_DIM
WINDOW = 128
CHUNK = 128
SSD_HEADS = GROUP_WIDTH // 64
SSD_STATE = 128
SSD_GROUPS = 2
SSD_BC = SSD_GROUPS * SSD_STATE
SSD_CONV = 5
SSD_CONV_CH = GROUP_WIDTH + 2 * SSD_BC
FFN_DIM = ((8 * D_MODEL // 3 + 255) // 256) * 256
FFN_CONV = 3
ROPE_THETA = 10000.0
RET_THETA = 10000.0
EPS = 1e-6
NEG_INF = -1e30
LOG2E = 1.4426950408889634
IN_COLS = 8 * GROUP_WIDTH + 4 * KV_WIDTH + 2 * SSD_BC + SSD_HEADS

IN_TILE = 768
IN_COLS_PAD = 7 * IN_TILE
COL_RET = 3
COL_SZ = 7
COL_XBC = 4
COL_DT = 40

LANES = 128
HALO = 16
MXU_DTYPE = jnp.bfloat16
VMEM_LIMIT = 56 * 2 ** 20

F32 = jnp.float32


def _cparams(*sem):
    return pltpu.CompilerParams(dimension_semantics=sem, vmem_limit_bytes=VMEM_LIMIT)


def _mm(a, b):
    return jnp.dot(a.astype(MXU_DTYPE), b.astype(MXU_DTYPE), preferred_element_type=F32)


def _mm_nt(a, b):
    return lax.dot_general(a.astype(MXU_DTYPE), b.astype(MXU_DTYPE), (((1,), (1,)), ((), ())),
                           preferred_element_type=F32)


def _silu(x):
    return x * jax.nn.sigmoid(x)


def _softplus(x):
    return jnp.maximum(x, 0.0) + jnp.log1p(jnp.exp(-jnp.abs(x)))


def _log_sigmoid(x):
    return jnp.minimum(x, 0.0) - jnp.log1p(jnp.exp(-jnp.abs(x)))


def _rms(x):
    return x * lax.rsqrt(jnp.mean(x * x, axis=-1, keepdims=True) + EPS)


def _ada_kernel(c_ref, w_ref, b_ref, o_ref):
    o_ref[0] = jnp.dot(_silu(c_ref[...]), w_ref[0], preferred_element_type=F32) + b_ref[0]


def _ada(cvec, w_ada, b_ada):
    depth, d, n = w_ada.shape
    tn = 1024
    return pl.pallas_call(
        _ada_kernel,
        grid=(depth, n // tn),
        in_specs=[pl.BlockSpec((8, d), lambda l, j: (0, 0)),
                  pl.BlockSpec((1, d, tn), lambda l, j: (l, 0, j)),
                  pl.BlockSpec((1, 1, tn), lambda l, j: (l, 0, j))],
        out_specs=pl.BlockSpec((1, 8, tn), lambda l, j: (l, 0, j)),
        out_shape=jax.ShapeDtypeStruct((depth, 8, n), F32),
        compiler_params=_cparams("parallel", "parallel"),
        name="ada",
    )(cvec, w_ada, b_ada.reshape(depth, 1, n))


def _inproj_kernel(x_ref, g_ref, sc_ref, sh_ref, w_ref, o_ref, h_sc):
    @pl.when(pl.program_id(2) == 0)
    def _():
        h = _rms(x_ref[0]) * g_ref[...] * (1.0 + sc_ref[0]) + sh_ref[0]
        h_sc[...] = h.astype(h_sc.dtype)

    o_ref[0] = jnp.dot(h_sc[...], w_ref[...], preferred_element_type=F32)


def _in_proj(x, g, sc, sh, w):
    nb, lx, d = x.shape
    tm = min(512, lx)
    return pl.pallas_call(
        _inproj_kernel,
        grid=(nb, lx // tm, IN_COLS_PAD // IN_TILE),
        in_specs=[pl.BlockSpec((1, tm, d), lambda b, i, j: (b, i, 0)),
                  pl.BlockSpec((1, d), lambda b, i, j: (0, 0)),
                  pl.BlockSpec((1, 1, d), lambda b, i, j: (b, 0, 0)),
                  pl.BlockSpec((1, 1, d), lambda b, i, j: (b, 0, 0)),
                  pl.BlockSpec((d, IN_TILE), lambda b, i, j: (0, j))],
        out_specs=pl.BlockSpec((1, tm, IN_TILE), lambda b, i, j: (b, i, j)),
        out_shape=jax.ShapeDtypeStruct((nb, lx, IN_COLS_PAD), F32),
        scratch_shapes=[pltpu.VMEM((tm, d), MXU_DTYPE)],
        compiler_params=_cparams("parallel", "parallel", "arbitrary"),
        name="in_proj",
    )(x, g, sc, sh, w)


def _rope_apply(x, cos, sa, sb, shift):
    return x * cos + pltpu.roll(x, LANES - shift, 1) * sa + pltpu.roll(x, shift, 1) * sb


def _qkv_prep_kernel(p_ref, cos_ref, sa_ref, sb_ref, gq_ref, gk_ref, bd_ref, q_ref, kt_ref, v_ref, *,
                     norm, rope, qscale):
    x = p_ref[0]

    def prep(t, g):
        if norm:
            sq = t * t
            hi = sq.astype(jnp.bfloat16)
            lo = (sq - hi.astype(F32)).astype(jnp.bfloat16)
            ss = (jnp.dot(hi, bd_ref[...], preferred_element_type=F32)
                  + jnp.dot(lo, bd_ref[...], preferred_element_type=F32))
            t = t * lax.rsqrt(ss * (1.0 / HEAD_DIM) + EPS) * g
        if rope:
            t = _rope_apply(t, cos_ref[...], sa_ref[...], sb_ref[...], HEAD_DIM // 4)
        return t

    for j in range(N_HEADS // 2):
        r = prep(x[:, j * LANES:(j + 1) * LANES], gq_ref[...]) * qscale
        q_ref[0, 2 * j] = r[:, :HEAD_DIM].astype(q_ref.dtype)
        q_ref[0, 2 * j + 1] = r[:, HEAD_DIM:].astype(q_ref.dtype)
    kt = prep(x[:, GROUP_WIDTH:GROUP_WIDTH + KV_WIDTH], gk_ref[...]).T
    kt_ref[0, 0] = kt[:HEAD_DIM].astype(kt_ref.dtype)
    kt_ref[0, 1] = kt[HEAD_DIM:].astype(kt_ref.dtype)
    v = x[:, GROUP_WIDTH + KV_WIDTH:]
    lane = lax.broadcasted_iota(jnp.int32, v.shape, 1)
    ones_col = (lane == HEAD_DIM).astype(F32)
    v_ref[0, 0] = jnp.where(lane < HEAD_DIM, v, ones_col).astype(v_ref.dtype)
    v_ref[0, 1] = jnp.where(lane < HEAD_DIM, pltpu.roll(v, HEAD_DIM, 1), ones_col).astype(v_ref.dtype)


def _qkv_prep(p, col, tabs, gq, gk, bd, *, norm, rope, qscale):
    nb, lx, _ = p.shape
    tl = min(256, lx)
    cos, sa, sb = tabs
    kern = functools.partial(_qkv_prep_kernel, norm=norm, rope=rope, qscale=qscale)
    tab_spec = pl.BlockSpec((tl, LANES), lambda b, i: (i, 0))
    row_spec = pl.BlockSpec((1, LANES), lambda b, i: (0, 0))
    return pl.pallas_call(
        kern,
        grid=(nb, lx // tl),
        in_specs=[pl.BlockSpec((1, tl, IN_TILE), lambda b, i: (b, i, col)),
                  tab_spec, tab_spec, tab_spec, row_spec, row_spec,
                  pl.BlockSpec((LANES, LANES), lambda b, i: (0, 0))],
        out_specs=[pl.BlockSpec((1, N_HEADS, tl, HEAD_DIM), lambda b, i: (b, 0, i, 0)),
                   pl.BlockSpec((1, KV_HEADS, HEAD_DIM, tl), lambda b, i: (b, 0, 0, i)),
                   pl.BlockSpec((1, KV_HEADS, tl, LANES), lambda b, i: (b, 0, i, 0))],
        out_shape=[jax.ShapeDtypeStruct((nb, N_HEADS, lx, HEAD_DIM), MXU_DTYPE),
                   jax.ShapeDtypeStruct((nb, KV_HEADS, HEAD_DIM, lx), MXU_DTYPE),
                   jax.ShapeDtypeStruct((nb, KV_HEADS, lx, LANES), MXU_DTYPE)],
        compiler_params=_cparams("parallel", "parallel"),
        name="qkv_prep",
    )(p, cos, sa, sb, gq, gk, bd)


def _flash_kernel(*refs, has_sink, tq, tk, nk):
    if has_sink:
        q_ref, kt_ref, v_ref, sink_ref, o_ref, s_sc, p_sc, a_sc, m_sc, acc_sc = refs
    else:
        q_ref, kt_ref, v_ref, o_ref, s_sc, p_sc, a_sc, m_sc, acc_sc = refs
    rows = Q_PER_KV * tq
    if has_sink:
        m_sc[...] = sink_ref[0] * LOG2E
        lane = lax.broadcasted_iota(jnp.int32, acc_sc.shape, 1)
        acc_sc[...] = (lane == HEAD_DIM).astype(F32)
    else:
        m_sc[...] = jnp.full_like(m_sc, -jnp.inf)
        acc_sc[...] = jnp.zeros_like(acc_sc)
    q = q_ref[0].reshape(rows, HEAD_DIM)

    def scores(j, slot):
        off = pl.multiple_of(j * tk, tk)
        s_sc[slot] = jnp.dot(q, kt_ref[0, 0, :, pl.ds(off, tk)], preferred_element_type=F32)

    def softmax(slot):
        s = s_sc[slot]
        m_prev = m_sc[...]
        m_new = jnp.maximum(m_prev, jnp.max(s, axis=1, keepdims=True))
        a_sc[slot] = jnp.exp2(m_prev - m_new)
        p_sc[slot] = jnp.exp2(s - m_new[:, :1]).astype(p_sc.dtype)
        m_sc[...] = m_new

    def values(j, slot):
        off = pl.multiple_of(j * tk, tk)
        pv = jnp.dot(p_sc[slot], v_ref[0, 0, pl.ds(off, tk), :], preferred_element_type=F32)
        acc_sc[...] = acc_sc[...] * a_sc[slot] + pv

    def stage(j, slot, n_after):
        values(j, slot)
        if n_after >= 1:
            softmax(1 - slot)
        if n_after >= 2:
            scores(j + 2, slot)

    scores(0, 0)
    softmax(0)
    if nk > 1:
        scores(1, 1)
    pairs = max(nk - 2, 0) // 2

    def body(jj, carry):
        stage(2 * jj, 0, 2)
        stage(2 * jj + 1, 1, 2)
        return carry

    lax.fori_loop(0, pairs, body, 0)
    for j in range(2 * pairs, nk):
        stage(j, j % 2, nk - 1 - j)

    acc = acc_sc[...]
    o = acc[:, :HEAD_DIM] / acc[:, HEAD_DIM:HEAD_DIM + 1]
    for h in range(Q_PER_KV):
        o_ref[0, :, h * HEAD_DIM:(h + 1) * HEAD_DIM] = o[h * tq:(h + 1) * tq].astype(o_ref.dtype)


def _flash(q, kt, v, sink_rows=None, *, tq, tk):
    nb, _, lq, _ = q.shape
    lk = kt.shape[-1]
    rows = Q_PER_KV * tq
    has_sink = sink_rows is not None
    in_specs = [pl.BlockSpec((1, Q_PER_KV, tq, HEAD_DIM), lambda b, k, i: (b, k, i, 0)),
                pl.BlockSpec((1, 1, HEAD_DIM, lk), lambda b, k, i: (b, k, 0, 0)),
                pl.BlockSpec((1, 1, lk, LANES), lambda b, k, i: (b, k, 0, 0))]
    args = [q, kt, v]
    if has_sink:
        in_specs.append(pl.BlockSpec((1, rows, LANES), lambda b, k, i: (k, 0, 0)))
        args.append(sink_rows)
    return pl.pallas_call(
        functools.partial(_flash_kernel, has_sink=has_sink, tq=tq, tk=tk, nk=lk // tk),
        grid=(nb, KV_HEADS, lq // tq),
        in_specs=in_specs,
        out_specs=pl.BlockSpec((1, tq, Q_PER_KV * HEAD_DIM), lambda b, k, i: (b, i, k)),
        out_shape=jax.ShapeDtypeStruct((nb, lq, GROUP_WIDTH), MXU_DTYPE),
        scratch_shapes=[pltpu.VMEM((2, rows, tk), F32), pltpu.VMEM((2, rows, tk), MXU_DTYPE),
                        pltpu.VMEM((2, rows, LANES), F32), pltpu.VMEM((rows, LANES), F32),
                        pltpu.VMEM((rows, LANES), F32)],
        compiler_params=_cparams("parallel", "parallel", "parallel"),
        name="flash_attn",
    )(*args)


def _window_kernel(q_ref, ktp_ref, ktc_ref, ktn_ref, ktx_ref, vp_ref, vc_ref, vn_ref, vx_ref, sink_ref, o_ref):
    i = pl.program_id(2)
    nblk = pl.num_programs(2)
    rows = Q_PER_KV * WINDOW
    q = q_ref[0].reshape(rows, HEAD_DIM)
    sp = jnp.dot(q, ktp_ref[0, 0], preferred_element_type=F32)
    sc = jnp.dot(q, ktc_ref[0, 0], preferred_element_type=F32)
    sn = jnp.dot(q, ktn_ref[0, 0], preferred_element_type=F32)
    sx = jnp.dot(q, ktx_ref[0, 0], preferred_element_type=F32)
    qpos = lax.broadcasted_iota(jnp.int32, (rows, WINDOW), 0) % WINDOW
    kpos = lax.broadcasted_iota(jnp.int32, (rows, WINDOW), 1)
    rel = kpos - qpos
    lo = jnp.where(i > 0, 0, 2 * WINDOW)
    hi = jnp.where(i < nblk - 1, 0, -2 * WINDOW)
    sp = jnp.where(rel >= lo, sp, NEG_INF)
    sn = jnp.where(rel <= hi, sn, NEG_INF)
    sink = sink_ref[0] * LOG2E
    m = jnp.maximum(jnp.maximum(jnp.max(sp, axis=1, keepdims=True), jnp.max(sc, axis=1, keepdims=True)),
                    jnp.maximum(jnp.max(sn, axis=1, keepdims=True), jnp.max(sx, axis=1, keepdims=True)))
    m = jnp.maximum(m, sink)
    m1 = m[:, :1]
    pp, pc, pn, px = jnp.exp2(sp - m1), jnp.exp2(sc - m1), jnp.exp2(sn - m1), jnp.exp2(sx - m1)
    dt = vp_ref.dtype
    o = (jnp.dot(pp.astype(dt), vp_ref[0, 0], preferred_element_type=F32)
         + jnp.dot(pc.astype(dt), vc_ref[0, 0], preferred_element_type=F32)
         + jnp.dot(pn.astype(dt), vn_ref[0, 0], preferred_element_type=F32)
         + jnp.dot(px.astype(dt), vx_ref[0, 0], preferred_element_type=F32))
    den = o + jnp.exp2(sink - m)
    o = o[:, :HEAD_DIM] / den[:, HEAD_DIM:HEAD_DIM + 1]
    for h in range(Q_PER_KV):
        o_ref[0, :, h * HEAD_DIM:(h + 1) * HEAD_DIM] = o[h * WINDOW:(h + 1) * WINDOW].astype(o_ref.dtype)


def _window(q, kt, v, ktx, vx, sink_rows):
    nb, _, lq, _ = q.shape
    lc = ktx.shape[-1]
    nblk = lq // WINDOW
    rows = Q_PER_KV * WINDOW

    def kspec(off):
        return pl.BlockSpec((1, 1, HEAD_DIM, WINDOW),
                            lambda b, k, i: (b, k, 0, jnp.clip(i + off, 0, nblk - 1)))

    def vspec(off):
        return pl.BlockSpec((1, 1, WINDOW, LANES),
                            lambda b, k, i: (b, k, jnp.clip(i + off, 0, nblk - 1), 0))

    return pl.pallas_call(
        _window_kernel,
        grid=(nb, KV_HEADS, nblk),
        in_specs=[pl.BlockSpec((1, Q_PER_KV, WINDOW, HEAD_DIM), lambda b, k, i: (b, k, i, 0)),
                  kspec(-1), kspec(0), kspec(1),
                  pl.BlockSpec((1, 1, HEAD_DIM, lc), lambda b, k, i: (b, k, 0, 0)),
                  vspec(-1), vspec(0), vspec(1),
                  pl.BlockSpec((1, 1, lc, LANES), lambda b, k, i: (b, k, 0, 0)),
                  pl.BlockSpec((1, rows, LANES), lambda b, k, i: (k, 0, 0))],
        out_specs=pl.BlockSpec((1, WINDOW, Q_PER_KV * HEAD_DIM), lambda b, k, i: (b, i, k)),
        out_shape=jax.ShapeDtypeStruct((nb, lq, GROUP_WIDTH), MXU_DTYPE),
        compiler_params=_cparams("parallel", "parallel", "parallel"),
        name="window_attn",
    )(q, kt, kt, kt, ktx, v, v, v, vx, sink_rows)


def _retention_kernel(*refs, reverse, rope, final):
    refs = list(refs)
    q_ref, k_ref, v_ref = refs[:3]
    del refs[:3]
    if rope:
        cos_ref, sa_ref, sb_ref = refs[:3]
        del refs[:3]
    lg_ref, lgw_ref, s0_ref = refs[:3]
    del refs[:3]
    if final:
        of_ref, gate_ref, gn_ref = refs[:3]
        del refs[:3]
    o_ref, sfin_ref, s_sc, d_sc = refs
    c = CHUNK
    n = pl.program_id(1)

    @pl.when(n == 0)
    def _():
        s_sc[...] = s0_ref[0]
        ri = lax.broadcasted_iota(jnp.int32, (c, c), 0)
        ci = lax.broadcasted_iota(jnp.int32, (c, c), 1)
        rel = (ci - ri if reverse else ri - ci).astype(F32)
        for h in range(N_HEADS):
            ld = _log_sigmoid(lg_ref[h])
            d_sc[h] = jnp.where(rel >= 0, jnp.exp(jnp.maximum(rel, 0.0) * ld), 0.0)

    q, k, v = q_ref[0], k_ref[0], v_ref[0]
    if rope:
        cos, sa, sb = cos_ref[...], sa_ref[...], sb_ref[...]
        q = jnp.concatenate([_rope_apply(q[:, j * LANES:(j + 1) * LANES], cos, sa, sb, HEAD_DIM // 2)
                             for j in range(GROUP_WIDTH // LANES)], axis=1)
        k = jnp.concatenate([_rope_apply(k[:, j * LANES:(j + 1) * LANES], cos, sa, sb, HEAD_DIM // 2)
                             for j in range(GROUP_WIDTH // LANES)], axis=1)
    k = k * (HEAD_DIM ** -0.5)
    ldw = _log_sigmoid(lgw_ref[...])
    idx = lax.broadcasted_iota(jnp.int32, (c, GROUP_WIDTH), 0).astype(F32)
    if reverse:
        q_in = q * jnp.exp((c - idx) * ldw)
        k_out = k * jnp.exp(idx * ldw)
    else:
        q_in = q * jnp.exp((idx + 1.0) * ldw)
        k_out = k * jnp.exp((c - 1.0 - idx) * ldw)

    outs = []
    for h in range(N_HEADS):
        sl = slice(h * HEAD_DIM, (h + 1) * HEAD_DIM)
        s = _mm_nt(q[:, sl], k[:, sl]) * d_sc[h]
        o = _mm(s, v[:, sl]) + _mm(q_in[:, sl], s_sc[h])
        kv = lax.dot_general(k_out[:, sl].astype(MXU_DTYPE), v[:, sl].astype(MXU_DTYPE),
                             (((0,), (0,)), ((), ())), preferred_element_type=F32)
        chunk_decay = jnp.exp(c * _log_sigmoid(lg_ref[h]))[:, :HEAD_DIM]
        s_sc[h] = s_sc[h] * chunk_decay + kv
        if final:
            o = o + of_ref[0][:, sl]
            mu = jnp.mean(o, axis=1, keepdims=True)
            var = jnp.mean(jnp.square(o - mu), axis=1, keepdims=True)
            o = (o - mu) * lax.rsqrt(var + EPS)
        outs.append(o)
    o = jnp.concatenate(outs, axis=1)
    if final:
        o = o * gn_ref[...] * _silu(gate_ref[0])
    o_ref[0] = o.astype(o_ref.dtype)

    @pl.when(n == pl.num_programs(1) - 1)
    def _():
        sfin_ref[0] = s_sc[...]


def _retention(p, tabs, lg, lgw, s0, *, reverse, o_fwd=None, gn=None):
    nb, lx, _ = p.shape
    c = CHUNK
    nc = lx // c
    rope = tabs is not None
    final = o_fwd is not None

    def cm(n):
        return nc - 1 - n if reverse else n

    def col(j):
        return pl.BlockSpec((1, c, GROUP_WIDTH), lambda b, n: (b, cm(n), j))

    in_specs = [col(COL_RET), col(COL_RET + 1), col(COL_RET + 2)]
    args = [p, p, p]
    if rope:
        in_specs += [pl.BlockSpec((c, LANES), lambda b, n: (cm(n), 0))] * 3
        args += list(tabs)
    in_specs += [pl.BlockSpec((N_HEADS, 1, LANES), lambda b, n: (0, 0, 0)),
                 pl.BlockSpec((1, GROUP_WIDTH), lambda b, n: (0, 0)),
                 pl.BlockSpec((1, N_HEADS, HEAD_DIM, HEAD_DIM), lambda b, n: (b, 0, 0, 0))]
    args += [lg, lgw, s0]
    if final:
        in_specs += [pl.BlockSpec((1, c, GROUP_WIDTH), lambda b, n: (b, cm(n), 0)), col(COL_RET + 3),
                     pl.BlockSpec((1, GROUP_WIDTH), lambda b, n: (0, 0))]
        args += [o_fwd, p, gn]
    return pl.pallas_call(
        functools.partial(_retention_kernel, reverse=reverse, rope=rope, final=final),
        grid=(nb, nc),
        in_specs=in_specs,
        out_specs=[pl.BlockSpec((1, c, GROUP_WIDTH), lambda b, n: (b, cm(n), 0)),
                   pl.BlockSpec((1, N_HEADS, HEAD_DIM, HEAD_DIM), lambda b, n: (b, 0, 0, 0))],
        out_shape=[jax.ShapeDtypeStruct((nb, lx, GROUP_WIDTH), MXU_DTYPE if final else F32),
                   jax.ShapeDtypeStruct((nb, N_HEADS, HEAD_DIM, HEAD_DIM), F32)],
        scratch_shapes=[pltpu.VMEM((N_HEADS, HEAD_DIM, HEAD_DIM), F32), pltpu.VMEM((N_HEADS, c, c), F32)],
        compiler_params=_cparams("parallel", "arbitrary"),
        name="retention",
    )(*args)


def _ssd_conv_kernel(xp_ref, xc_ref, xn_ref, w_ref, b_ref, o_ref, x_sc):
    i = pl.program_id(1)
    tl = xc_ref.shape[1]
    x_sc[0:HALO] = jnp.where(i > 0, xp_ref[0], 0.0)
    x_sc[HALO:HALO + tl] = xc_ref[0]
    x_sc[HALO + tl:] = jnp.where(i < pl.num_programs(1) - 1, xn_ref[0], 0.0)
    acc = jnp.zeros((tl, xc_ref.shape[2]), F32) + b_ref[...]
    for k in range(SSD_CONV):
        off = HALO + k - SSD_CONV // 2
        acc = acc + x_sc[off:off + tl] * w_ref[k:k + 1]
    o_ref[0] = _silu(acc)


def _ssd_conv(p, w, b):
    nb, lx, _ = p.shape
    tl = min(256, lx)
    nh = lx // HALO
    r = tl // HALO
    ch = SSD_CONV_CH
    col = COL_XBC
    return pl.pallas_call(
        _ssd_conv_kernel,
        grid=(nb, lx // tl),
        in_specs=[pl.BlockSpec((1, HALO, ch), lambda b_, i: (b_, jnp.maximum(i * r - 1, 0), col)),
                  pl.BlockSpec((1, tl, ch), lambda b_, i: (b_, i, col)),
                  pl.BlockSpec((1, HALO, ch), lambda b_, i: (b_, jnp.minimum((i + 1) * r, nh - 1), col)),
                  pl.BlockSpec((SSD_CONV, ch), lambda b_, i: (0, 0)),
                  pl.BlockSpec((1, ch), lambda b_, i: (0, 0))],
        out_specs=pl.BlockSpec((1, tl, ch), lambda b_, i: (b_, i, 0)),
        out_shape=jax.ShapeDtypeStruct((nb, lx, ch), F32),
        scratch_shapes=[pltpu.VMEM((tl + 2 * HALO, ch), F32)],
        compiler_params=_cparams("parallel", "parallel"),
        name="ssd_conv",
    )(p, p, p, w, b)


def _ssd_kernel(*refs, reverse, final):
    refs = list(refs)
    xbc_ref, dt_ref, dtt_ref, bias_row_ref, alog_row_ref, bias_col_ref, alog_col_ref, h0_ref = refs[:8]
    del refs[:8]
    if final:
        yf_ref, z_ref, dskip_ref, ng_ref = refs[:4]
        del refs[:4]
    y_ref, hfin_ref, h_sc = refs
    c = CHUNK
    n = pl.program_id(1)
    hp = lax.Precision.HIGHEST

    @pl.when(n == 0)
    def _():
        h_sc[...] = h0_ref[0]

    ri = lax.broadcasted_iota(jnp.int32, (c, c), 0)
    ci = lax.broadcasted_iota(jnp.int32, (c, c), 1)
    tri_b = (ri <= ci) if reverse else (ri >= ci)
    tri = tri_b.astype(F32)

    dt = _softplus(dt_ref[0] + bias_row_ref[...])
    a = dt * (-jnp.exp(alog_row_ref[...]))
    a_cum = jnp.dot(tri, a, precision=hp, preferred_element_type=F32)
    dtt = _softplus(dtt_ref[0] + bias_col_ref[...])
    at = dtt * (-jnp.exp(alog_col_ref[...]))
    a_cum_t = lax.dot_general(at, tri, (((1,), (1,)), ((), ())), precision=hp, preferred_element_type=F32)
    a_tot_t = jnp.dot(at, jnp.ones((c, c), F32), precision=hp, preferred_element_type=F32)

    xbc = xbc_ref[0]
    x = xbc[:, :GROUP_WIDTH]
    xt = x.T
    outs = []
    for g in range(SSD_GROUPS):
        bm = xbc[:, GROUP_WIDTH + g * SSD_STATE:GROUP_WIDTH + (g + 1) * SSD_STATE]
        cmat = xbc[:, GROUP_WIDTH + SSD_BC + g * SSD_STATE:GROUP_WIDTH + SSD_BC + (g + 1) * SSD_STATE]
        cb = _mm_nt(cmat, bm)
        for hh in range(SSD_HEADS // SSD_GROUPS):
            h = g * (SSD_HEADS // SSD_GROUPS) + hh
            sl = slice(h * 64, (h + 1) * 64)
            col = a_cum[:, h:h + 1]
            row = a_cum_t[h:h + 1, :]
            dt_row = dtt[h:h + 1, :]
            tot_row = a_tot_t[h:h + 1, :]
            decay = jnp.exp(jnp.where(tri_b, col - row, NEG_INF))
            w = cb * decay * dt_row
            y = _mm(w, x[:, sl]) + _mm_nt(cmat, h_sc[h]) * jnp.exp(col)
            to_end = jnp.exp(tot_row - row) * dt_row
            st = _mm(xt[sl, :] * to_end, bm)
            h_sc[h] = h_sc[h] * jnp.exp(tot_row) + st
            outs.append(y)
    y = jnp.concatenate(outs, axis=1)
    if final:
        y = (yf_ref[0] + y + dskip_ref[...] * x) * _silu(z_ref[0])
        half = GROUP_WIDTH // SSD_GROUPS
        y = jnp.concatenate([_rms(y[:, :half]), _rms(y[:, half:])], axis=1) * ng_ref[...]
    y_ref[0] = y.astype(y_ref.dtype)

    @pl.when(n == pl.num_programs(1) - 1)
    def _():
        hfin_ref[0] = h_sc[...]


def _ssd(xbc, p, dtt, prm, h0, *, reverse, y_fwd=None, dskip=None, ng=None):
    nb, lx, _ = xbc.shape
    c = CHUNK
    nc = lx // c
    final = y_fwd is not None

    def cm(n):
        return nc - 1 - n if reverse else n

    row = pl.BlockSpec((1, LANES), lambda b, n: (0, 0))
    colb = pl.BlockSpec((SSD_HEADS, LANES), lambda b, n: (0, 0))
    wide = pl.BlockSpec((1, GROUP_WIDTH), lambda b, n: (0, 0))
    state = pl.BlockSpec((1, SSD_HEADS, 64, SSD_STATE), lambda b, n: (b, 0, 0, 0))
    in_specs = [pl.BlockSpec((1, c, SSD_CONV_CH), lambda b, n: (b, cm(n), 0)),
                pl.BlockSpec((1, c, LANES), lambda b, n: (b, cm(n), COL_DT)),
                pl.BlockSpec((1, SSD_HEADS, c), lambda b, n: (b, 0, cm(n))),
                row, row, colb, colb, state]
    args = [xbc, p, dtt, *prm, h0]
    if final:
        in_specs += [pl.BlockSpec((1, c, GROUP_WIDTH), lambda b, n: (b, cm(n), 0)),
                     pl.BlockSpec((1, c, GROUP_WIDTH), lambda b, n: (b, cm(n), COL_SZ)), wide, wide]
        args += [y_fwd, p, dskip, ng]
    return pl.pallas_call(
        functools.partial(_ssd_kernel, reverse=reverse, final=final),
        grid=(nb, nc),
        in_specs=in_specs,
        out_specs=[pl.BlockSpec((1, c, GROUP_WIDTH), lambda b, n: (b, cm(n), 0)), state],
        out_shape=[jax.ShapeDtypeStruct((nb, lx, GROUP_WIDTH), MXU_DTYPE if final else F32),
                   jax.ShapeDtypeStruct((nb, SSD_HEADS, 64, SSD_STATE), F32)],
        scratch_shapes=[pltpu.VMEM((SSD_HEADS, 64, SSD_STATE), F32)],
        compiler_params=_cparams("parallel", "arbitrary"),
        name="ssd_scan",
    )(*args)


def _outproj_kernel(a_ref, w_ref_in, r_ref, s_ref, x_ref, g_ref, w_ref, o_ref):
    acc = jnp.zeros(o_ref.shape[1:], F32)
    for m, ref in enumerate((a_ref, w_ref_in, r_ref, s_ref)):
        acc = acc + jnp.dot(ref[0], w_ref[m * GROUP_WIDTH:(m + 1) * GROUP_WIDTH, :], preferred_element_type=F32)
    o_ref[0] = x_ref[0] + g_ref[0] * acc


def _out_proj(mix, x, g1, w):
    nb, lx, d = x.shape
    tm = min(512, lx)
    mspec = pl.BlockSpec((1, tm, GROUP_WIDTH), lambda b, i: (b, i, 0))
    return pl.pallas_call(
        _outproj_kernel,
        grid=(nb, lx // tm),
        in_specs=[mspec, mspec, mspec, mspec,
                  pl.BlockSpec((1, tm, d), lambda b, i: (b, i, 0)),
                  pl.BlockSpec((1, 1, d), lambda b, i: (b, 0, 0)),
                  pl.BlockSpec((4 * GROUP_WIDTH, d), lambda b, i: (0, 0))],
        out_specs=pl.BlockSpec((1, tm, d), lambda b, i: (b, i, 0)),
        out_shape=jax.ShapeDtypeStruct((nb, lx, d), F32),
        compiler_params=_cparams("parallel", "parallel"),
        name="out_proj",
    )(*mix, x, g1, w)


def _ffn_kernel(*refs, final):
    refs = list(refs)
    (xp_ref, xc_ref, xn_ref, g_ref, sc_ref, sh_ref, gate_ref, wg_ref, wv_ref, cwg_ref, cwv_ref, cbg_ref, cbv_ref,
     wd_ref) = refs[:14]
    del refs[:14]
    if final:
        fg_ref = refs.pop(0)
    o_ref, h_sc, acc_sc, ug_sc, uv_sc = refs
    i, j = pl.program_id(1), pl.program_id(2)
    tm = xc_ref.shape[1]

    @pl.when(j == 0)
    def _():
        def nm(t):
            return (_rms(t) * g_ref[...] * (1.0 + sc_ref[0]) + sh_ref[0]).astype(h_sc.dtype)
        h_sc[0:HALO] = nm(xp_ref[0])
        h_sc[HALO:HALO + tm] = nm(xc_ref[0])
        h_sc[HALO + tm:] = nm(xn_ref[0])
        acc_sc[...] = jnp.zeros_like(acc_sc)

    h = h_sc[...]
    ug_sc[...] = jnp.dot(h, wg_ref[...], preferred_element_type=F32)
    uv_sc[...] = jnp.dot(h, wv_ref[...], preferred_element_type=F32)

    @pl.when(i == 0)
    def _():
        ug_sc[0:HALO] = jnp.zeros((HALO, ug_sc.shape[1]), F32)
        uv_sc[0:HALO] = jnp.zeros((HALO, uv_sc.shape[1]), F32)

    @pl.when(i == pl.num_programs(1) - 1)
    def _():
        ug_sc[HALO + tm:] = jnp.zeros((HALO, ug_sc.shape[1]), F32)
        uv_sc[HALO + tm:] = jnp.zeros((HALO, uv_sc.shape[1]), F32)

    def conv(u_sc, w_ref, b_ref):
        acc = jnp.zeros((tm, u_sc.shape[1]), F32) + b_ref[...]
        for k in range(FFN_CONV):
            off = HALO + k - FFN_CONV // 2
            acc = acc + u_sc[off:off + tm] * w_ref[k:k + 1]
        return acc

    act = _silu(conv(ug_sc, cwg_ref, cbg_ref)) * conv(uv_sc, cwv_ref, cbv_ref)
    acc_sc[...] += jnp.dot(act.astype(wd_ref.dtype), wd_ref[...], preferred_element_type=F32)

    @pl.when(j == pl.num_programs(2) - 1)
    def _():
        y = xc_ref[0] + gate_ref[0] * acc_sc[...]
        if final:
            y = _rms(y) * fg_ref[...]
        o_ref[0] = y


def _ffn(x, g, sc, sh, gate, w_up, conv_w, conv_b, w_down, final_g=None):
    nb, lx, d = x.shape
    f = w_down.shape[0]
    tm = min(512, lx)
    tf = 512
    nf = f // tf
    nh = lx // HALO
    r = tm // HALO
    final = final_g is not None
    vec = pl.BlockSpec((1, d), lambda b, i, j: (0, 0))
    mod = pl.BlockSpec((1, 1, d), lambda b, i, j: (b, 0, 0))
    in_specs = [pl.BlockSpec((1, HALO, d), lambda b, i, j: (b, jnp.maximum(i * r - 1, 0), 0)),
                pl.BlockSpec((1, tm, d), lambda b, i, j: (b, i, 0)),
                pl.BlockSpec((1, HALO, d), lambda b, i, j: (b, jnp.minimum((i + 1) * r, nh - 1), 0)),
                vec, mod, mod, mod,
                pl.BlockSpec((d, tf), lambda b, i, j: (0, j)),
                pl.BlockSpec((d, tf), lambda b, i, j: (0, nf + j)),
                pl.BlockSpec((FFN_CONV, tf), lambda b, i, j: (0, j)),
                pl.BlockSpec((FFN_CONV, tf), lambda b, i, j: (0, nf + j)),
                pl.BlockSpec((1, tf), lambda b, i, j: (0, j)),
                pl.BlockSpec((1, tf), lambda b, i, j: (0, nf + j)),
                pl.BlockSpec((tf, d), lambda b, i, j: (j, 0))]
    args = [x, x, x, g, sc, sh, gate, w_up, w_up, conv_w, conv_w, conv_b, conv_b, w_down]
    if final:
        in_specs.append(vec)
        args.append(final_g)
    return pl.pallas_call(
        functools.partial(_ffn_kernel, final=final),
        grid=(nb, lx // tm, nf),
        in_specs=in_specs,
        out_specs=pl.BlockSpec((1, tm, d), lambda b, i, j: (b, i, 0)),
        out_shape=jax.ShapeDtypeStruct((nb, lx, d), F32),
        scratch_shapes=[pltpu.VMEM((tm + 2 * HALO, d), MXU_DTYPE), pltpu.VMEM((tm, d), F32),
                        pltpu.VMEM((tm + 2 * HALO, tf), F32), pltpu.VMEM((tm + 2 * HALO, tf), F32)],
        compiler_params=_cparams("parallel", "parallel", "arbitrary"),
        name="conv_ffn",
    )(*args)


def _pair_tables(ang_blocks):
    cos, sa, sb = [], [], []
    for ang in ang_blocks:
        c, s = jnp.cos(ang), jnp.sin(ang)
        z = jnp.zeros_like(s)
        cos += [c, c]
        sa += [-s, z]
        sb += [z, s]
    def two(parts):
        t = jnp.concatenate(parts, axis=-1)
        return jnp.concatenate([t, t], axis=-1)
    return two(cos), two(sa), two(sb)


def _lane_rep(v, width):
    return jnp.repeat(v, width)[None, :]


def kernel(x, c, ctx, c_ctx, w_ada, b_ada, norm1_g, w_in, attn_qn_g, attn_kn_g, win_sink, ret_decay_logit,
           ret_gn_g, ssd_conv_w, ssd_conv_b, ssd_a_log, ssd_dt_bias, ssd_d, ssd_norm_g, w_out, norm2_g,
           ffn_w_up, ffn_conv_w, ffn_conv_b, ffn_w_down, final_g):
    b, l, d = x.shape
    lc = ctx.shape[1]
    depth = w_ada.shape[0]
    assert d == D_MODEL and l % 512 == 0 and lc % 256 == 0 and b + 1 <= 8

    rows = l // GRID_W
    row = jnp.repeat(jnp.arange(rows), GRID_W).astype(F32)
    colp = jnp.tile(jnp.arange(GRID_W), rows).astype(F32)
    n_ax = HEAD_DIM // 4
    inv_ax = ROPE_THETA ** (-jnp.arange(n_ax, dtype=F32) / n_ax)
    tabs_2d = _pair_tables([row[:, None] * inv_ax, colp[:, None] * inv_ax])
    n_ret = HEAD_DIM // 2
    inv_ret = RET_THETA ** (-jnp.arange(n_ret, dtype=F32) / n_ret)
    tabs_ret = _pair_tables([jnp.arange(l, dtype=F32)[:, None] * inv_ret])
    tabs_ctx = (jnp.ones((lc, LANES), F32), jnp.zeros((lc, LANES), F32), jnp.zeros((lc, LANES), F32))
    bd = (jnp.arange(LANES)[:, None] // HEAD_DIM == jnp.arange(LANES)[None, :] // HEAD_DIM).astype(jnp.bfloat16)

    cvec = jnp.zeros((8, d), F32).at[:b].set(c).at[b].set(c_ctx)
    mod_all = _ada(cvec, w_ada, b_ada)

    qscale = HEAD_DIM ** -0.5 * LOG2E
    xc = ctx
    for i in range(depth):
        need_ctx = i < depth - 1
        mod = mod_all[i].reshape(8, 6, 1, d)
        sh1, sc1, g1, sh2, sc2, g2 = [mod[:b, j] for j in range(6)]
        sh1c, sc1c, g1c, sh2c, sc2c, g2c = [jnp.broadcast_to(mod[b:b + 1, j], (b, 1, d)) for j in range(6)]

        w_in_p = jnp.pad(w_in[i], ((0, 0), (0, IN_COLS_PAD - IN_COLS))).astype(MXU_DTYPE)
        n1 = norm1_g[i][None, :]
        p = _in_proj(x, n1, sc1, sh1, w_in_p)
        pc = _in_proj(xc, n1, sc1c, sh1c, w_in_p)

        gq = jnp.tile(attn_qn_g[i], 2)[None, :]
        gk = jnp.tile(attn_kn_g[i], 2)[None, :]
        qa, kta, va = _qkv_prep(p, 0, tabs_2d, gq, gk, bd, norm=True, rope=True, qscale=qscale)
        qac, ktac, vac = _qkv_prep(pc, 0, tabs_ctx, gq, gk, bd, norm=True, rope=False, qscale=qscale)
        qw, ktw, vw = _qkv_prep(p, 1, tabs_2d, gq, gk, bd, norm=False, rope=True, qscale=qscale)
        qwc, ktwc, vwc = _qkv_prep(pc, 1, tabs_ctx, gq, gk, bd, norm=False, rope=False, qscale=qscale)
        sink = win_sink[i].reshape(KV_HEADS, Q_PER_KV)

        def sink_rows(t):
            return jnp.broadcast_to(sink[:, :, None, None], (KV_HEADS, Q_PER_KV, t, LANES)).reshape(
                KV_HEADS, Q_PER_KV * t, LANES)

        tk = (l + lc) // 13 if (l + lc) % (13 * 256) == 0 else 128
        o_att = _flash(qa, jnp.concatenate([kta, ktac], axis=-1), jnp.concatenate([va, vac], axis=2),
                       tq=256, tk=tk)
        o_win = _window(qw, ktw, vw, ktwc, vwc, sink_rows(WINDOW))

        lg = jnp.broadcast_to(ret_decay_logit[i][:, :, None, None], (2, N_HEADS, 1, LANES))
        lgw = [_lane_rep(ret_decay_logit[i][dr], HEAD_DIM) for dr in range(2)]
        s0 = jnp.zeros((b, N_HEADS, HEAD_DIM, HEAD_DIM), F32)
        gn = ret_gn_g[i][None, :]
        oc_f, sc_f = _retention(pc, None, lg[0], lgw[0], s0, reverse=False)
        o_ret_c, sc_b = _retention(pc, None, lg[1], lgw[1], s0, reverse=True, o_fwd=oc_f, gn=gn)
        o_f, _ = _retention(p, tabs_ret, lg[0], lgw[0], sc_f, reverse=False)
        o_ret, _ = _retention(p, tabs_ret, lg[1], lgw[1], sc_b, reverse=True, o_fwd=o_f, gn=gn)

        conv_b = ssd_conv_b[i][None, :]
        xbc = _ssd_conv(p, ssd_conv_w[i], conv_b)
        xbcc = _ssd_conv(pc, ssd_conv_w[i], conv_b)
        dtt = jnp.swapaxes(p[:, :, COL_DT * LANES:COL_DT * LANES + SSD_HEADS], 1, 2)
        dttc = jnp.swapaxes(pc[:, :, COL_DT * LANES:COL_DT * LANES + SSD_HEADS], 1, 2)

        def ssd_prm(dr):
            pad = jnp.zeros((LANES - SSD_HEADS,), F32)
            return (jnp.concatenate([ssd_dt_bias[i][dr], pad])[None, :],
                    jnp.concatenate([ssd_a_log[i][dr], pad])[None, :],
                    jnp.broadcast_to(ssd_dt_bias[i][dr][:, None], (SSD_HEADS, LANES)),
                    jnp.broadcast_to(ssd_a_log[i][dr][:, None], (SSD_HEADS, LANES)))

        h0 = jnp.zeros((b, SSD_HEADS, 64, SSD_STATE), F32)
        dskip = _lane_rep(ssd_d[i], 64)
        ng = ssd_norm_g[i][None, :]
        yc_f, hc_f = _ssd(xbcc, pc, dttc, ssd_prm(0), h0, reverse=False)
        o_ssd_c, hc_b = _ssd(xbcc, pc, dttc, ssd_prm(1), h0, reverse=True, y_fwd=yc_f, dskip=dskip, ng=ng)
        y_f, _ = _ssd(xbc, p, dtt, ssd_prm(0), hc_f, reverse=False)
        o_ssd, _ = _ssd(xbc, p, dtt, ssd_prm(1), hc_b, reverse=True, y_fwd=y_f, dskip=dskip, ng=ng)

        w_out_b = w_out[i].astype(MXU_DTYPE)
        w_up_b = ffn_w_up[i].astype(MXU_DTYPE)
        w_down_b = ffn_w_down[i].astype(MXU_DTYPE)
        n2 = norm2_g[i][None, :]
        cb2 = ffn_conv_b[i][None, :]
        x = _out_proj((o_att, o_win, o_ret, o_ssd), x, g1, w_out_b)
        x = _ffn(x, n2, sc2, sh2, g2, w_up_b, ffn_conv_w[i], cb2, w_down_b,
                 final_g=None if need_ctx else final_g[None, :])
        if need_ctx:
            o_att_c = _flash(qac, ktac, vac, tq=lc, tk=lc)
            o_win_c = _flash(qwc, ktwc, vwc, sink_rows(lc), tq=lc, tk=lc)
            xc = _out_proj((o_att_c, o_win_c, o_ret_c, o_ssd_c), xc, g1c, w_out_b)
            xc = _ffn(xc, n2, sc2c, sh2c, g2c, w_up_b, ffn_conv_w[i], cb2, w_down_b)
    return x
```

```python
import functools
import math

import jax
import jax.numpy as jnp
from jax import lax
from jax.experimental import pallas as pl
from jax.experimental.pallas import tpu as pltpu

D_MODEL = 2048
GRID_W = 64
HEAD_DIM = 64
GROUP_WIDTH = D_MODEL // 4
N_HEADS = GROUP_WIDTH // HEAD_DIM
KV_HEADS = 2
Q_PER_KV = N_HEADS // KV_HEADS
KV_WIDTH = KV_HEADS * HEAD_DIM
WINDOW = 128
CHUNK = 128
SSD_HEADS = GROUP_WIDTH // 64
SSD_STATE = 128
SSD_GROUPS = 2
SSD_BC = SSD_GROUPS * SSD_STATE
SSD_CONV = 5
SSD_CONV_CH = GROUP_WIDTH + 2 * SSD_BC
FFN_DIM = ((8 * D_MODEL // 3 + 255) // 256) * 256
FFN_CONV = 3
ROPE_THETA = 10000.0
RET_THETA = 10000.0
EPS = 1e-6
NEG_INF = -1e30
LOG2E = 1.4426950408889634
IN_COLS = 8 * GROUP_WIDTH + 4 * KV_WIDTH + 2 * SSD_BC + SSD_HEADS

IN_TILE = 768
IN_COLS_PAD = 7 * IN_TILE
COL_RET = 3
COL_SZ = 7
COL_XBC = 4
COL_DT = 40

LANES = 128
HALO = 16
FFN_PARTS = 2
MXU_DTYPE = jnp.bfloat16
VMEM_LIMIT = 56 * 2 ** 20

F32 = jnp.float32


def _cparams(*sem):
    return pltpu.CompilerParams(dimension_semantics=sem, vmem_limit_bytes=VMEM_LIMIT)


def _mm(a, b):
    return jnp.dot(a.astype(MXU_DTYPE), b.astype(MXU_DTYPE), preferred_element_type=F32)


def _mm_nt(a, b):
    return lax.dot_general(a.astype(MXU_DTYPE), b.astype(MXU_DTYPE), (((1,), (1,)), ((), ())),
                           preferred_element_type=F32)


def _silu(x):
    return x * jax.nn.sigmoid(x)


def _softplus(x):
    return jnp.maximum(x, 0.0) + jnp.log1p(jnp.exp(-jnp.abs(x)))


def _log_sigmoid(x):
    return jnp.minimum(x, 0.0) - jnp.log1p(jnp.exp(-jnp.abs(x)))


def _rms(x):
    return x * lax.rsqrt(jnp.mean(x * x, axis=-1, keepdims=True) + EPS)


def _ada_kernel(c_ref, w_ref, b_ref, o_ref):
    o_ref[0] = jnp.dot(_silu(c_ref[...]), w_ref[0], preferred_element_type=F32) + b_ref[0]


def _ada(cvec, w_ada, b_ada):
    depth, d, n = w_ada.shape
    tn = 1024
    return pl.pallas_call(
        _ada_kernel,
        grid=(depth, n // tn),
        in_specs=[pl.BlockSpec((8, d), lambda l, j: (0, 0)),
                  pl.BlockSpec((1, d, tn), lambda l, j: (l, 0, j)),
                  pl.BlockSpec((1, 1, tn), lambda l, j: (l, 0, j))],
        out_specs=pl.BlockSpec((1, 8, tn), lambda l, j: (l, 0, j)),
        out_shape=jax.ShapeDtypeStruct((depth, 8, n), F32),
        compiler_params=_cparams("parallel", "parallel"),
        name="ada",
    )(cvec, w_ada, b_ada.reshape(depth, 1, n))


def _inproj_kernel(x_ref, g_ref, sc_ref, sh_ref, w_ref, o_ref, h_sc):
    @pl.when(pl.program_id(2) == 0)
    def _():
        h = _rms(x_ref[0]) * g_ref[...] * (1.0 + sc_ref[0]) + sh_ref[0]
        h_sc[...] = h.astype(h_sc.dtype)

    o_ref[0] = jnp.dot(h_sc[...], w_ref[...], preferred_element_type=F32)


def _in_proj(x, g, sc, sh, w):
    nb, lx, d = x.shape
    tm = 1024 if lx % 1024 == 0 else min(512, lx)
    return pl.pallas_call(
        _inproj_kernel,
        grid=(nb, lx // tm, IN_COLS_PAD // IN_TILE),
        in_specs=[pl.BlockSpec((1, tm, d), lambda b, i, j: (b, i, 0)),
                  pl.BlockSpec((1, d), lambda b, i, j: (0, 0)),
                  pl.BlockSpec((1, 1, d), lambda b, i, j: (b, 0, 0)),
                  pl.BlockSpec((1, 1, d), lambda b, i, j: (b, 0, 0)),
                  pl.BlockSpec((d, IN_TILE), lambda b, i, j: (0, j))],
        out_specs=pl.BlockSpec((1, tm, IN_TILE), lambda b, i, j: (b, i, j)),
        out_shape=jax.ShapeDtypeStruct((nb, lx, IN_COLS_PAD), F32),
        scratch_shapes=[pltpu.VMEM((tm, d), MXU_DTYPE)],
        compiler_params=_cparams("parallel", "parallel", "arbitrary"),
        name="in_proj",
    )(x, g, sc, sh, w)


def _rope_apply(x, cos, sa, sb, shift):
    return x * cos + pltpu.roll(x, LANES - shift, 1) * sa + pltpu.roll(x, shift, 1) * sb


def _qkv_prep_kernel(p_ref, cos_ref, sa_ref, sb_ref, gq_ref, gk_ref, bd_ref, q_ref, kt_ref, v_ref, *,
                     norm, rope, qscale):
    x = p_ref[0]

    def prep(t, g):
        if norm:
            sq = t * t
            hi = sq.astype(jnp.bfloat16)
            lo = (sq - hi.astype(F32)).astype(jnp.bfloat16)
            ss = (jnp.dot(hi, bd_ref[...], preferred_element_type=F32)
                  + jnp.dot(lo, bd_ref[...], preferred_element_type=F32))
            t = t * lax.rsqrt(ss * (1.0 / HEAD_DIM) + EPS) * g
        if rope:
            t = _rope_apply(t, cos_ref[...], sa_ref[...], sb_ref[...], HEAD_DIM // 4)
        return t

    for j in range(N_HEADS // 2):
        r = prep(x[:, j * LANES:(j + 1) * LANES], gq_ref[...]) * qscale
        q_ref[0, 2 * j] = r[:, :HEAD_DIM].astype(q_ref.dtype)
        q_ref[0, 2 * j + 1] = r[:, HEAD_DIM:].astype(q_ref.dtype)
    kt = prep(x[:, GROUP_WIDTH:GROUP_WIDTH + KV_WIDTH], gk_ref[...]).T
    kt_ref[0, 0] = kt[:HEAD_DIM].astype(kt_ref.dtype)
    kt_ref[0, 1] = kt[HEAD_DIM:].astype(kt_ref.dtype)
    v = x[:, GROUP_WIDTH + KV_WIDTH:]
    lane = lax.broadcasted_iota(jnp.int32, v.shape, 1)
    ones_col = (lane == HEAD_DIM).astype(F32)
    v_ref[0, 0] = jnp.where(lane < HEAD_DIM, v, ones_col).astype(v_ref.dtype)
    v_ref[0, 1] = jnp.where(lane < HEAD_DIM, pltpu.roll(v, HEAD_DIM, 1), ones_col).astype(v_ref.dtype)


def _qkv_prep(p, col, tabs, gq, gk, bd, *, norm, rope, qscale):
    nb, lx, _ = p.shape
    tl = min(256, lx)
    cos, sa, sb = tabs
    kern = functools.partial(_qkv_prep_kernel, norm=norm, rope=rope, qscale=qscale)
    tab_spec = pl.BlockSpec((tl, LANES), lambda b, i: (i, 0))
    row_spec = pl.BlockSpec((1, LANES), lambda b, i: (0, 0))
    return pl.pallas_call(
        kern,
        grid=(nb, lx // tl),
        in_specs=[pl.BlockSpec((1, tl, IN_TILE), lambda b, i: (b, i, col)),
                  tab_spec, tab_spec, tab_spec, row_spec, row_spec,
                  pl.BlockSpec((LANES, LANES), lambda b, i: (0, 0))],
        out_specs=[pl.BlockSpec((1, N_HEADS, tl, HEAD_DIM), lambda b, i: (b, 0, i, 0)),
                   pl.BlockSpec((1, KV_HEADS, HEAD_DIM, tl), lambda b, i: (b, 0, 0, i)),
                   pl.BlockSpec((1, KV_HEADS, tl, LANES), lambda b, i: (b, 0, i, 0))],
        out_shape=[jax.ShapeDtypeStruct((nb, N_HEADS, lx, HEAD_DIM), MXU_DTYPE),
                   jax.ShapeDtypeStruct((nb, KV_HEADS, HEAD_DIM, lx), MXU_DTYPE),
                   jax.ShapeDtypeStruct((nb, KV_HEADS, lx, LANES), MXU_DTYPE)],
        compiler_params=_cparams("parallel", "parallel"),
        name="qkv_prep",
    )(p, cos, sa, sb, gq, gk, bd)


def _flash_kernel(*refs, has_sink, tq, tk, nk):
    if has_sink:
        q_ref, kt_ref, v_ref, sink_ref, o_ref, s_sc, p_sc, a_sc, m_sc, acc_sc = refs
    else:
        q_ref, kt_ref, v_ref, o_ref, s_sc, p_sc, a_sc, m_sc, acc_sc = refs
    rows = Q_PER_KV * tq
    if has_sink:
        m_sc[...] = sink_ref[0] * LOG2E
        lane = lax.broadcasted_iota(jnp.int32, acc_sc.shape, 1)
        acc_sc[...] = (lane == HEAD_DIM).astype(F32)
    else:
        m_sc[...] = jnp.full_like(m_sc, -jnp.inf)
        acc_sc[...] = jnp.zeros_like(acc_sc)
    q = q_ref[0].reshape(rows, HEAD_DIM)

    def scores(j, slot):
        off = pl.multiple_of(j * tk, tk)
        s_sc[slot] = jnp.dot(q, kt_ref[0, 0, :, pl.ds(off, tk)], preferred_element_type=F32)

    def softmax(slot):
        s = s_sc[slot]
        m_prev = m_sc[...]
        m_new = jnp.maximum(m_prev, jnp.max(s, axis=1, keepdims=True))
        a_sc[slot] = jnp.exp2(m_prev - m_new)
        p_sc[slot] = jnp.exp2(s - m_new[:, :1]).astype(p_sc.dtype)
        m_sc[...] = m_new

    def values(j, slot):
        off = pl.multiple_of(j * tk, tk)
        pv = jnp.dot(p_sc[slot], v_ref[0, 0, pl.ds(off, tk), :], preferred_element_type=F32)
        acc_sc[...] = acc_sc[...] * a_sc[slot] + pv

    def stage(j, slot, n_after):
        values(j, slot)
        if n_after >= 1:
            softmax(1 - slot)
        if n_after >= 2:
            scores(j + 2, slot)

    scores(0, 0)
    softmax(0)
    if nk > 1:
        scores(1, 1)
    pairs = max(nk - 2, 0) // 2

    def body(jj, carry):
        stage(2 * jj, 0, 2)
        stage(2 * jj + 1, 1, 2)
        return carry

    lax.fori_loop(0, pairs, body, 0)
    for j in range(2 * pairs, nk):
        stage(j, j % 2, nk - 1 - j)

    acc = acc_sc[...]
    o = acc[:, :HEAD_DIM] / acc[:, HEAD_DIM:HEAD_DIM + 1]
    for h in range(Q_PER_KV):
        o_ref[0, :, h * HEAD_DIM:(h + 1) * HEAD_DIM] = o[h * tq:(h + 1) * tq].astype(o_ref.dtype)


def _flash(q, kt, v, sink_rows=None, *, tq, tk):
    nb, _, lq, _ = q.shape
    lk = kt.shape[-1]
    rows = Q_PER_KV * tq
    has_sink = sink_rows is not None
    in_specs = [pl.BlockSpec((1, Q_PER_KV, tq, HEAD_DIM), lambda b, k, i: (b, k, i, 0)),
                pl.BlockSpec((1, 1, HEAD_DIM, lk), lambda b, k, i: (b, k, 0, 0)),
                pl.BlockSpec((1, 1, lk, LANES), lambda b, k, i: (b, k, 0, 0))]
    args = [q, kt, v]
    if has_sink:
        in_specs.append(pl.BlockSpec((1, rows, LANES), lambda b, k, i: (k, 0, 0)))
        args.append(sink_rows)
    return pl.pallas_call(
        functools.partial(_flash_kernel, has_sink=has_sink, tq=tq, tk=tk, nk=lk // tk),
        grid=(nb, KV_HEADS, lq // tq),
        in_specs=in_specs,
        out_specs=pl.BlockSpec((1, tq, Q_PER_KV * HEAD_DIM), lambda b, k, i: (b, i, k)),
        out_shape=jax.ShapeDtypeStruct((nb, lq, GROUP_WIDTH), MXU_DTYPE),
        scratch_shapes=[pltpu.VMEM((2, rows, tk), F32), pltpu.VMEM((2, rows, tk), MXU_DTYPE),
                        pltpu.VMEM((2, rows, LANES), F32), pltpu.VMEM((rows, LANES), F32),
                        pltpu.VMEM((rows, LANES), F32)],
        compiler_params=_cparams("parallel", "parallel", "parallel"),
        name="flash_attn",
    )(*args)


def _window_kernel(q_ref, ktp_ref, ktc_ref, ktn_ref, ktx_ref, vp_ref, vc_ref, vn_ref, vx_ref, sink_ref, o_ref):
    i = pl.program_id(2)
    nblk = pl.num_programs(2)
    rows = Q_PER_KV * WINDOW
    q = q_ref[0].reshape(rows, HEAD_DIM)
    sp = jnp.dot(q, ktp_ref[0, 0], preferred_element_type=F32)
    sc = jnp.dot(q, ktc_ref[0, 0], preferred_element_type=F32)
    sn = jnp.dot(q, ktn_ref[0, 0], preferred_element_type=F32)
    sx = jnp.dot(q, ktx_ref[0, 0], preferred_element_type=F32)
    qpos = lax.broadcasted_iota(jnp.int32, (rows, WINDOW), 0) % WINDOW
    kpos = lax.broadcasted_iota(jnp.int32, (rows, WINDOW), 1)
    rel = kpos - qpos
    lo = jnp.where(i > 0, 0, 2 * WINDOW)
    hi = jnp.where(i < nblk - 1, 0, -2 * WINDOW)
    sp = jnp.where(rel >= lo, sp, NEG_INF)
    sn = jnp.where(rel <= hi, sn, NEG_INF)
    sink = sink_ref[0] * LOG2E
    m = jnp.maximum(jnp.maximum(jnp.max(sp, axis=1, keepdims=True), jnp.max(sc, axis=1, keepdims=True)),
                    jnp.maximum(jnp.max(sn, axis=1, keepdims=True), jnp.max(sx, axis=1, keepdims=True)))
    m = jnp.maximum(m, sink)
    m1 = m[:, :1]
    pp, pc, pn, px = jnp.exp2(sp - m1), jnp.exp2(sc - m1), jnp.exp2(sn - m1), jnp.exp2(sx - m1)
    dt = vp_ref.dtype
    o = (jnp.dot(pp.astype(dt), vp_ref[0, 0], preferred_element_type=F32)
         + jnp.dot(pc.astype(dt), vc_ref[0, 0], preferred_element_type=F32)
         + jnp.dot(pn.astype(dt), vn_ref[0, 0], preferred_element_type=F32)
         + jnp.dot(px.astype(dt), vx_ref[0, 0], preferred_element_type=F32))
    den = o + jnp.exp2(sink - m)
    o = o[:, :HEAD_DIM] / den[:, HEAD_DIM:HEAD_DIM + 1]
    for h in range(Q_PER_KV):
        o_ref[0, :, h * HEAD_DIM:(h + 1) * HEAD_DIM] = o[h * WINDOW:(h + 1) * WINDOW].astype(o_ref.dtype)


def _window(q, kt, v, ktx, vx, sink_rows):
    nb, _, lq, _ = q.shape
    lc = ktx.shape[-1]
    nblk = lq // WINDOW
    rows = Q_PER_KV * WINDOW

    def kspec(off):
        return pl.BlockSpec((1, 1, HEAD_DIM, WINDOW),
                            lambda b, k, i: (b, k, 0, jnp.clip(i + off, 0, nblk - 1)))

    def vspec(off):
        return pl.BlockSpec((1, 1, WINDOW, LANES),
                            lambda b, k, i: (b, k, jnp.clip(i + off, 0, nblk - 1), 0))

    return pl.pallas_call(
        _window_kernel,
        grid=(nb, KV_HEADS, nblk),
        in_specs=[pl.BlockSpec((1, Q_PER_KV, WINDOW, HEAD_DIM), lambda b, k, i: (b, k, i, 0)),
                  kspec(-1), kspec(0), kspec(1),
                  pl.BlockSpec((1, 1, HEAD_DIM, lc), lambda b, k, i: (b, k, 0, 0)),
                  vspec(-1), vspec(0), vspec(1),
                  pl.BlockSpec((1, 1, lc, LANES), lambda b, k, i: (b, k, 0, 0)),
                  pl.BlockSpec((1, rows, LANES), lambda b, k, i: (k, 0, 0))],
        out_specs=pl.BlockSpec((1, WINDOW, Q_PER_KV * HEAD_DIM), lambda b, k, i: (b, i, k)),
        out_shape=jax.ShapeDtypeStruct((nb, lq, GROUP_WIDTH), MXU_DTYPE),
        compiler_params=_cparams("parallel", "parallel", "parallel"),
        name="window_attn",
    )(q, kt, kt, kt, ktx, v, v, v, vx, sink_rows)


def _retention_kernel(*refs, reverse, rope, final):
    refs = list(refs)
    q_ref, k_ref, v_ref = refs[:3]
    del refs[:3]
    if rope:
        cos_ref, sa_ref, sb_ref = refs[:3]
        del refs[:3]
    lg_ref, lgw_ref, s0_ref = refs[:3]
    del refs[:3]
    if final:
        of_ref, gate_ref, gn_ref = refs[:3]
        del refs[:3]
    o_ref, sfin_ref, s_sc, d_sc = refs
    c = CHUNK
    n = pl.program_id(0)
    nb = q_ref.shape[0]
    pairs = N_HEADS // 2
    lane = lax.broadcasted_iota(jnp.int32, (c, LANES), 1)
    first = lane < HEAD_DIM
    blk = (lax.broadcasted_iota(jnp.int32, (LANES, LANES), 0) < HEAD_DIM) == (
        lax.broadcasted_iota(jnp.int32, (LANES, LANES), 1) < HEAD_DIM)

    @pl.when(n == 0)
    def _():
        for b in range(nb):
            for t in range(pairs):
                s_sc[b, t] = jnp.zeros((LANES, LANES), F32)
                s_sc[b, t, 0:HEAD_DIM, 0:HEAD_DIM] = s0_ref[b, 2 * t]
                s_sc[b, t, HEAD_DIM:, HEAD_DIM:] = s0_ref[b, 2 * t + 1]
        ri = lax.broadcasted_iota(jnp.int32, (c, c), 0)
        ci = lax.broadcasted_iota(jnp.int32, (c, c), 1)
        rel = (ci - ri if reverse else ri - ci).astype(F32)
        for h in range(N_HEADS):
            ld = _log_sigmoid(lg_ref[h])
            d_sc[h] = jnp.where(rel >= 0, jnp.exp(jnp.maximum(rel, 0.0) * ld), 0.0)

    ldw = _log_sigmoid(lgw_ref[...])
    idx = lax.broadcasted_iota(jnp.int32, (c, GROUP_WIDTH), 0).astype(F32)
    if reverse:
        q_scale = jnp.exp((c - idx) * ldw)
        k_scale = jnp.exp(idx * ldw)
    else:
        q_scale = jnp.exp((idx + 1.0) * ldw)
        k_scale = jnp.exp((c - 1.0 - idx) * ldw)
    chunk_decay = jnp.exp(c * ldw)

    for b in range(nb):
        q, k, v = q_ref[b], k_ref[b], v_ref[b]
        outs = []
        for t in range(pairs):
            sl = slice(t * LANES, (t + 1) * LANES)
            qp, kp, vp = q[:, sl], k[:, sl], v[:, sl]
            if rope:
                qp = _rope_apply(qp, cos_ref[...], sa_ref[...], sb_ref[...], HEAD_DIM // 2)
                kp = _rope_apply(kp, cos_ref[...], sa_ref[...], sb_ref[...], HEAD_DIM // 2)
            kp = kp * (HEAD_DIM ** -0.5)
            kb, vb = kp.astype(MXU_DTYPE), vp.astype(MXU_DTYPE)
            s0 = _mm_nt(jnp.where(first, qp, 0.0), kb) * d_sc[2 * t]
            s1 = _mm_nt(jnp.where(first, 0.0, qp), kb) * d_sc[2 * t + 1]
            o = jnp.where(first, _mm(s0, vb), _mm(s1, vb))
            o = o + _mm(qp * q_scale[:, sl], s_sc[b, t])
            kv = lax.dot_general((kp * k_scale[:, sl]).astype(MXU_DTYPE), vb,
                                 (((0,), (0,)), ((), ())), preferred_element_type=F32)
            s_sc[b, t] = s_sc[b, t] * chunk_decay[:, sl] + jnp.where(blk, kv, 0.0)
            if final:
                o = o + of_ref[b][:, sl]
                inv = 1.0 / HEAD_DIM
                mu0 = jnp.sum(jnp.where(first, o, 0.0), axis=1, keepdims=True) * inv
                mu1 = jnp.sum(jnp.where(first, 0.0, o), axis=1, keepdims=True) * inv
                dlt = o - jnp.where(first, mu0, mu1)
                sq = dlt * dlt
                var0 = jnp.sum(jnp.where(first, sq, 0.0), axis=1, keepdims=True) * inv
                var1 = jnp.sum(jnp.where(first, 0.0, sq), axis=1, keepdims=True) * inv
                o = dlt * lax.rsqrt(jnp.where(first, var0, var1) + EPS)
            outs.append(o)
        o = jnp.concatenate(outs, axis=1)
        if final:
            o = o * gn_ref[...] * _silu(gate_ref[b])
        o_ref[b] = o.astype(o_ref.dtype)

    @pl.when(n == pl.num_programs(0) - 1)
    def _():
        for b in range(nb):
            for t in range(pairs):
                sfin_ref[b, 2 * t] = s_sc[b, t, 0:HEAD_DIM, 0:HEAD_DIM]
                sfin_ref[b, 2 * t + 1] = s_sc[b, t, HEAD_DIM:, HEAD_DIM:]


def _retention(p, tabs, lg, lgw, s0, *, reverse, o_fwd=None, gn=None):
    nb, lx, _ = p.shape
    c = CHUNK
    nc = lx // c
    rope = tabs is not None
    final = o_fwd is not None

    def cm(n):
        return nc - 1 - n if reverse else n

    def col(j):
        return pl.BlockSpec((nb, c, GROUP_WIDTH), lambda n: (0, cm(n), j))

    state = pl.BlockSpec((nb, N_HEADS, HEAD_DIM, HEAD_DIM), lambda n: (0, 0, 0, 0))
    in_specs = [col(COL_RET), col(COL_RET + 1), col(COL_RET + 2)]
    args = [p, p, p]
    if rope:
        in_specs += [pl.BlockSpec((c, LANES), lambda n: (cm(n), 0))] * 3
        args += list(tabs)
    in_specs += [pl.BlockSpec((N_HEADS, 1, LANES), lambda n: (0, 0, 0)),
                 pl.BlockSpec((1, GROUP_WIDTH), lambda n: (0, 0)), state]
    args += [lg, lgw, s0]
    if final:
        in_specs += [col(0), col(COL_RET + 3), pl.BlockSpec((1, GROUP_WIDTH), lambda n: (0, 0))]
        args += [o_fwd, p, gn]
    return pl.pallas_call(
        functools.partial(_retention_kernel, reverse=reverse, rope=rope, final=final),
        grid=(nc,),
        in_specs=in_specs,
        out_specs=[col(0), state],
        out_shape=[jax.ShapeDtypeStruct((nb, lx, GROUP_WIDTH), MXU_DTYPE if final else F32),
                   jax.ShapeDtypeStruct((nb, N_HEADS, HEAD_DIM, HEAD_DIM), F32)],
        scratch_shapes=[pltpu.VMEM((nb, N_HEADS // 2, LANES, LANES), F32), pltpu.VMEM((N_HEADS, c, c), F32)],
        compiler_params=_cparams("arbitrary"),
        name="retention",
    )(*args)


def _ssd_conv_kernel(xp_ref, xc_ref, xn_ref, w_ref, b_ref, o_ref, x_sc):
    i = pl.program_id(1)
    tl = xc_ref.shape[1]
    x_sc[0:HALO] = jnp.where(i > 0, xp_ref[0], 0.0)
    x_sc[HALO:HALO + tl] = xc_ref[0]
    x_sc[HALO + tl:] = jnp.where(i < pl.num_programs(1) - 1, xn_ref[0], 0.0)
    acc = jnp.zeros((tl, xc_ref.shape[2]), F32) + b_ref[...]
    for k in range(SSD_CONV):
        off = HALO + k - SSD_CONV // 2
        acc = acc + x_sc[off:off + tl] * w_ref[k:k + 1]
    o_ref[0] = _silu(acc)


def _ssd_conv(p, w, b):
    nb, lx, _ = p.shape
    tl = min(256, lx)
    nh = lx // HALO
    r = tl // HALO
    ch = SSD_CONV_CH
    col = COL_XBC
    return pl.pallas_call(
        _ssd_conv_kernel,
        grid=(nb, lx // tl),
        in_specs=[pl.BlockSpec((1, HALO, ch), lambda b_, i: (b_, jnp.maximum(i * r - 1, 0), col)),
                  pl.BlockSpec((1, tl, ch), lambda b_, i: (b_, i, col)),
                  pl.BlockSpec((1, HALO, ch), lambda b_, i: (b_, jnp.minimum((i + 1) * r, nh - 1), col)),
                  pl.BlockSpec((SSD_CONV, ch), lambda b_, i: (0, 0)),
                  pl.BlockSpec((1, ch), lambda b_, i: (0, 0))],
        out_specs=pl.BlockSpec((1, tl, ch), lambda b_, i: (b_, i, 0)),
        out_shape=jax.ShapeDtypeStruct((nb, lx, ch), F32),
        scratch_shapes=[pltpu.VMEM((tl + 2 * HALO, ch), F32)],
        compiler_params=_cparams("parallel", "parallel"),
        name="ssd_conv",
    )(p, p, p, w, b)


def _ssd_kernel(*refs, reverse, final):
    refs = list(refs)
    xbc_ref, dt_ref, dtt_ref, bias_row_ref, alog_row_ref, bias_col_ref, alog_col_ref, h0_ref = refs[:8]
    del refs[:8]
    if final:
        yf_ref, z_ref, dskip_ref, ng_ref = refs[:4]
        del refs[:4]
    y_ref, hfin_ref, h_sc = refs
    c = CHUNK
    n = pl.program_id(0)
    nb = xbc_ref.shape[0]
    hp = lax.Precision.HIGHEST

    @pl.when(n == 0)
    def _():
        h_sc[...] = h0_ref[...]

    ri = lax.broadcasted_iota(jnp.int32, (c, c), 0)
    ci = lax.broadcasted_iota(jnp.int32, (c, c), 1)
    tri_b = (ri <= ci) if reverse else (ri >= ci)
    tri = tri_b.astype(F32)
    ones = jnp.ones((c, c), F32)

    for b in range(nb):
        dt = _softplus(dt_ref[b] + bias_row_ref[...])
        a = dt * (-jnp.exp(alog_row_ref[...]))
        a_cum = jnp.dot(tri, a, precision=hp, preferred_element_type=F32)
        dtt = _softplus(dtt_ref[b] + bias_col_ref[...])
        at = dtt * (-jnp.exp(alog_col_ref[...]))
        a_cum_t = lax.dot_general(at, tri, (((1,), (1,)), ((), ())), precision=hp, preferred_element_type=F32)
        a_tot_t = jnp.dot(at, ones, precision=hp, preferred_element_type=F32)

        xbc = xbc_ref[b]
        x = xbc[:, :GROUP_WIDTH]
        xt = x.T
        outs = []
        for g in range(SSD_GROUPS):
            bm = xbc[:, GROUP_WIDTH + g * SSD_STATE:GROUP_WIDTH + (g + 1) * SSD_STATE]
            cmat = xbc[:, GROUP_WIDTH + SSD_BC + g * SSD_STATE:GROUP_WIDTH + SSD_BC + (g + 1) * SSD_STATE]
            cb = _mm_nt(cmat, bm)
            for hh in range(SSD_HEADS // SSD_GROUPS):
                h = g * (SSD_HEADS // SSD_GROUPS) + hh
                sl = slice(h * 64, (h + 1) * 64)
                col = a_cum[:, h:h + 1]
                row = a_cum_t[h:h + 1, :]
                dt_row = dtt[h:h + 1, :]
                tot_row = a_tot_t[h:h + 1, :]
                decay = jnp.exp(jnp.where(tri_b, col - row, NEG_INF))
                w = cb * decay * dt_row
                y = _mm(w, x[:, sl]) + _mm_nt(cmat, h_sc[b, h]) * jnp.exp(col)
                to_end = jnp.exp(tot_row - row) * dt_row
                st = _mm(xt[sl, :] * to_end, bm)
                h_sc[b, h] = h_sc[b, h] * jnp.exp(tot_row) + st
                outs.append(y)
        y = jnp.concatenate(outs, axis=1)
        if final:
            y = (yf_ref[b] + y + dskip_ref[...] * x) * _silu(z_ref[b])
            half = GROUP_WIDTH // SSD_GROUPS
            y = jnp.concatenate([_rms(y[:, :half]), _rms(y[:, half:])], axis=1) * ng_ref[...]
        y_ref[b] = y.astype(y_ref.dtype)

    @pl.when(n == pl.num_programs(0) - 1)
    def _():
        hfin_ref[...] = h_sc[...]


def _ssd(xbc, p, dtt, prm, h0, *, reverse, y_fwd=None, dskip=None, ng=None):
    nb, lx, _ = xbc.shape
    c = CHUNK
    nc = lx // c
    final = y_fwd is not None

    def cm(n):
        return nc - 1 - n if reverse else n

    row = pl.BlockSpec((1, LANES), lambda n: (0, 0))
    colb = pl.BlockSpec((SSD_HEADS, LANES), lambda n: (0, 0))
    wide = pl.BlockSpec((1, GROUP_WIDTH), lambda n: (0, 0))
    state = pl.BlockSpec((nb, SSD_HEADS, 64, SSD_STATE), lambda n: (0, 0, 0, 0))
    yspec = pl.BlockSpec((nb, c, GROUP_WIDTH), lambda n: (0, cm(n), 0))
    in_specs = [pl.BlockSpec((nb, c, SSD_CONV_CH), lambda n: (0, cm(n), 0)),
                pl.BlockSpec((nb, c, LANES), lambda n: (0, cm(n), COL_DT)),
                pl.BlockSpec((nb, SSD_HEADS, c), lambda n: (0, 0, cm(n))),
                row, row, colb, colb, state]
    args = [xbc, p, dtt, *prm, h0]
    if final:
        in_specs += [yspec, pl.BlockSpec((nb, c, GROUP_WIDTH), lambda n: (0, cm(n), COL_SZ)), wide, wide]
        args += [y_fwd, p, dskip, ng]
    return pl.pallas_call(
        functools.partial(_ssd_kernel, reverse=reverse, final=final),
        grid=(nc,),
        in_specs=in_specs,
        out_specs=[yspec, state],
        out_shape=[jax.ShapeDtypeStruct((nb, lx, GROUP_WIDTH), MXU_DTYPE if final else F32),
                   jax.ShapeDtypeStruct((nb, SSD_HEADS, 64, SSD_STATE), F32)],
        scratch_shapes=[pltpu.VMEM((nb, SSD_HEADS, 64, SSD_STATE), F32)],
        compiler_params=_cparams("arbitrary"),
        name="ssd_scan",
    )(*args)


def _outproj_kernel(a_ref, w_ref_in, r_ref, s_ref, x_ref, g_ref, w_ref, o_ref):
    acc = jnp.zeros(o_ref.shape[1:], F32)
    for m, ref in enumerate((a_ref, w_ref_in, r_ref, s_ref)):
        acc = acc + jnp.dot(ref[0], w_ref[m * GROUP_WIDTH:(m + 1) * GROUP_WIDTH, :], preferred_element_type=F32)
    o_ref[0] = x_ref[0] + g_ref[0] * acc


def _out_proj(mix, x, g1, w):
    nb, lx, d = x.shape
    tm = min(512, lx)
    mspec = pl.BlockSpec((1, tm, GROUP_WIDTH), lambda b, i: (b, i, 0))
    return pl.pallas_call(
        _outproj_kernel,
        grid=(nb, lx // tm),
        in_specs=[mspec, mspec, mspec, mspec,
                  pl.BlockSpec((1, tm, d), lambda b, i: (b, i, 0)),
                  pl.BlockSpec((1, 1, d), lambda b, i: (b, 0, 0)),
                  pl.BlockSpec((4 * GROUP_WIDTH, d), lambda b, i: (0, 0))],
        out_specs=pl.BlockSpec((1, tm, d), lambda b, i: (b, i, 0)),
        out_shape=jax.ShapeDtypeStruct((nb, lx, d), F32),
        compiler_params=_cparams("parallel", "parallel"),
        name="out_proj",
    )(*mix, x, g1, w)


def _ffn_kernel(*refs, final):
    refs = list(refs)
    (xp_ref, xc_ref, xn_ref, g_ref, sc_ref, sh_ref, gate_ref, wg_ref, wv_ref, cwg_ref, cwv_ref, cbg_ref, cbv_ref,
     wd_ref) = refs[:14]
    del refs[:14]
    if final:
        fg_ref = refs.pop(0)
    o_ref, h_sc, acc_sc, ug_sc, uv_sc = refs
    i, j = pl.program_id(1), pl.program_id(2)
    tm = xc_ref.shape[1]

    @pl.when(j == 0)
    def _():
        def nm(t):
            return (_rms(t) * g_ref[...] * (1.0 + sc_ref[0]) + sh_ref[0]).astype(h_sc.dtype)
        h_sc[0:HALO] = nm(xp_ref[0])
        h_sc[HALO:HALO + tm] = nm(xc_ref[0])
        h_sc[HALO + tm:] = nm(xn_ref[0])
        acc_sc[...] = jnp.zeros_like(acc_sc)

    h = h_sc[...]
    keep_prev = jnp.where(i > 0, 1.0, 0.0)
    keep_next = jnp.where(i < pl.num_programs(1) - 1, 1.0, 0.0)
    parts = ug_sc.shape[0]
    th = ug_sc.shape[2]

    def conv(u_ref, w_ref, b_ref, cs):
        acc = jnp.zeros((tm, th), F32) + b_ref[:, cs]
        for k in range(FFN_CONV):
            off = HALO + k - FFN_CONV // 2
            acc = acc + u_ref[off:off + tm] * w_ref[k:k + 1, cs]
        return acc

    down = None
    for a in range(parts):
        cs = slice(a * th, (a + 1) * th)
        for w_ref, u_sc in ((wg_ref, ug_sc), (wv_ref, uv_sc)):
            u = jnp.dot(h, w_ref[:, cs], preferred_element_type=F32)
            u_sc[a, 0:HALO] = u[0:HALO] * keep_prev
            u_sc[a, HALO:HALO + tm] = u[HALO:HALO + tm]
            u_sc[a, HALO + tm:] = u[HALO + tm:] * keep_next
    for a in range(parts):
        cs = slice(a * th, (a + 1) * th)
        act = _silu(conv(ug_sc.at[a], cwg_ref, cbg_ref, cs)) * conv(uv_sc.at[a], cwv_ref, cbv_ref, cs)
        d = jnp.dot(act.astype(wd_ref.dtype), wd_ref[cs, :], preferred_element_type=F32)
        down = d if down is None else down + d
    acc_sc[...] += down

    @pl.when(j == pl.num_programs(2) - 1)
    def _():
        y = xc_ref[0] + gate_ref[0] * acc_sc[...]
        if final:
            y = _rms(y) * fg_ref[...]
        o_ref[0] = y


def _ffn(x, g, sc, sh, gate, w_up, conv_w, conv_b, w_down, final_g=None):
    nb, lx, d = x.shape
    f = w_down.shape[0]
    tm = min(512, lx)
    tf = 512
    nf = f // tf
    nh = lx // HALO
    r = tm // HALO
    final = final_g is not None
    vec = pl.BlockSpec((1, d), lambda b, i, j: (0, 0))
    mod = pl.BlockSpec((1, 1, d), lambda b, i, j: (b, 0, 0))
    in_specs = [pl.BlockSpec((1, HALO, d), lambda b, i, j: (b, jnp.maximum(i * r - 1, 0), 0)),
                pl.BlockSpec((1, tm, d), lambda b, i, j: (b, i, 0)),
                pl.BlockSpec((1, HALO, d), lambda b, i, j: (b, jnp.minimum((i + 1) * r, nh - 1), 0)),
                vec, mod, mod, mod,
                pl.BlockSpec((d, tf), lambda b, i, j: (0, j)),
                pl.BlockSpec((d, tf), lambda b, i, j: (0, nf + j)),
                pl.BlockSpec((FFN_CONV, tf), lambda b, i, j: (0, j)),
                pl.BlockSpec((FFN_CONV, tf), lambda b, i, j: (0, nf + j)),
                pl.BlockSpec((1, tf), lambda b, i, j: (0, j)),
                pl.BlockSpec((1, tf), lambda b, i, j: (0, nf + j)),
                pl.BlockSpec((tf, d), lambda b, i, j: (j, 0))]
    args = [x, x, x, g, sc, sh, gate, w_up, w_up, conv_w, conv_w, conv_b, conv_b, w_down]
    if final:
        in_specs.append(vec)
        args.append(final_g)
    return pl.pallas_call(
        functools.partial(_ffn_kernel, final=final),
        grid=(nb, lx // tm, nf),
        in_specs=in_specs,
        out_specs=pl.BlockSpec((1, tm, d), lambda b, i, j: (b, i, 0)),
        out_shape=jax.ShapeDtypeStruct((nb, lx, d), F32),
        scratch_shapes=[pltpu.VMEM((tm + 2 * HALO, d), MXU_DTYPE), pltpu.VMEM((tm, d), F32),
                        pltpu.VMEM((FFN_PARTS, tm + 2 * HALO, tf // FFN_PARTS), F32),
                        pltpu.VMEM((FFN_PARTS, tm + 2 * HALO, tf // FFN_PARTS), F32)],
        compiler_params=_cparams("parallel", "parallel", "arbitrary"),
        name="conv_ffn",
    )(*args)


def _pair_tables(ang_blocks):
    cos, sa, sb = [], [], []
    for ang in ang_blocks:
        c, s = jnp.cos(ang), jnp.sin(ang)
        z = jnp.zeros_like(s)
        cos += [c, c]
        sa += [-s, z]
        sb += [z, s]
    def two(parts):
        t = jnp.concatenate(parts, axis=-1)
        return jnp.concatenate([t, t], axis=-1)
    return two(cos), two(sa), two(sb)


def _lane_rep(v, width):
    return jnp.repeat(v, width)[None, :]


def kernel(x, c, ctx, c_ctx, w_ada, b_ada, norm1_g, w_in, attn_qn_g, attn_kn_g, win_sink, ret_decay_logit,
           ret_gn_g, ssd_conv_w, ssd_conv_b, ssd_a_log, ssd_dt_bias, ssd_d, ssd_norm_g, w_out, norm2_g,
           ffn_w_up, ffn_conv_w, ffn_conv_b, ffn_w_down, final_g):
    b, l, d = x.shape
    lc = ctx.shape[1]
    depth = w_ada.shape[0]
    assert d == D_MODEL and l % 512 == 0 and lc % 256 == 0 and b + 1 <= 8

    rows = l // GRID_W
    row = jnp.repeat(jnp.arange(rows), GRID_W).astype(F32)
    colp = jnp.tile(jnp.arange(GRID_W), rows).astype(F32)
    n_ax = HEAD_DIM // 4
    inv_ax = ROPE_THETA ** (-jnp.arange(n_ax, dtype=F32) / n_ax)
    tabs_2d = _pair_tables([row[:, None] * inv_ax, colp[:, None] * inv_ax])
    n_ret = HEAD_DIM // 2
    inv_ret = RET_THETA ** (-jnp.arange(n_ret, dtype=F32) / n_ret)
    tabs_ret = _pair_tables([jnp.arange(l, dtype=F32)[:, None] * inv_ret])
    tabs_ctx = (jnp.ones((lc, LANES), F32), jnp.zeros((lc, LANES), F32), jnp.zeros((lc, LANES), F32))
    bd = (jnp.arange(LANES)[:, None] // HEAD_DIM == jnp.arange(LANES)[None, :] // HEAD_DIM).astype(jnp.bfloat16)

    cvec = jnp.zeros((8, d), F32).at[:b].set(c).at[b].set(c_ctx)
    mod_all = _ada(cvec, w_ada, b_ada)

    qscale = HEAD_DIM ** -0.5 * LOG2E
    xc = ctx
    for i in range(depth):
        need_ctx = i < depth - 1
        mod = mod_all[i].reshape(8, 6, 1, d)
        sh1, sc1, g1, sh2, sc2, g2 = [mod[:b, j] for j in range(6)]
        sh1c, sc1c, g1c, sh2c, sc2c, g2c = [jnp.broadcast_to(mod[b:b + 1, j], (b, 1, d)) for j in range(6)]

        w_in_p = jnp.pad(w_in[i], ((0, 0), (0, IN_COLS_PAD - IN_COLS))).astype(MXU_DTYPE)
        n1 = norm1_g[i][None, :]
        p = _in_proj(x, n1, sc1, sh1, w_in_p)
        pc = _in_proj(xc, n1, sc1c, sh1c, w_in_p)

        gq = jnp.tile(attn_qn_g[i], 2)[None, :]
        gk = jnp.tile(attn_kn_g[i], 2)[None, :]
        qa, kta, va = _qkv_prep(p, 0, tabs_2d, gq, gk, bd, norm=True, rope=True, qscale=qscale)
        qac, ktac, vac = _qkv_prep(pc, 0, tabs_ctx, gq, gk, bd, norm=True, rope=False, qscale=qscale)
        qw, ktw, vw = _qkv_prep(p, 1, tabs_2d, gq, gk, bd, norm=False, rope=True, qscale=qscale)
        qwc, ktwc, vwc = _qkv_prep(pc, 1, tabs_ctx, gq, gk, bd, norm=False, rope=False, qscale=qscale)
        sink = win_sink[i].reshape(KV_HEADS, Q_PER_KV)

        def sink_rows(t):
            return jnp.broadcast_to(sink[:, :, None, None], (KV_HEADS, Q_PER_KV, t, LANES)).reshape(
                KV_HEADS, Q_PER_KV * t, LANES)

        tk = (l + lc) // 13 if (l + lc) % (13 * 256) == 0 else 128
        o_att = _flash(qa, jnp.concatenate([kta, ktac], axis=-1), jnp.concatenate([va, vac], axis=2),
                       tq=128, tk=tk)
        o_win = _window(qw, ktw, vw, ktwc, vwc, sink_rows(WINDOW))

        lg = jnp.broadcast_to(ret_decay_logit[i][:, :, None, None], (2, N_HEADS, 1, LANES))
        lgw = [_lane_rep(ret_decay_logit[i][dr], HEAD_DIM) for dr in range(2)]
        s0 = jnp.zeros((b, N_HEADS, HEAD_DIM, HEAD_DIM), F32)
        gn = ret_gn_g[i][None, :]
        oc_f, sc_f = _retention(pc, None, lg[0], lgw[0], s0, reverse=False)
        o_ret_c, sc_b = _retention(pc, None, lg[1], lgw[1], s0, reverse=True, o_fwd=oc_f, gn=gn)
        o_f, _ = _retention(p, tabs_ret, lg[0], lgw[0], sc_f, reverse=False)
        o_ret, _ = _retention(p, tabs_ret, lg[1], lgw[1], sc_b, reverse=True, o_fwd=o_f, gn=gn)

        conv_b = ssd_conv_b[i][None, :]
        xbc = _ssd_conv(p, ssd_conv_w[i], conv_b)
        xbcc = _ssd_conv(pc, ssd_conv_w[i], conv_b)
        dtt = jnp.swapaxes(p[:, :, COL_DT * LANES:COL_DT * LANES + SSD_HEADS], 1, 2)
        dttc = jnp.swapaxes(pc[:, :, COL_DT * LANES:COL_DT * LANES + SSD_HEADS], 1, 2)

        def ssd_prm(dr):
            pad = jnp.zeros((LANES - SSD_HEADS,), F32)
            return (jnp.concatenate([ssd_dt_bias[i][dr], pad])[None, :],
                    jnp.concatenate([ssd_a_log[i][dr], pad])[None, :],
                    jnp.broadcast_to(ssd_dt_bias[i][dr][:, None], (SSD_HEADS, LANES)),
                    jnp.broadcast_to(ssd_a_log[i][dr][:, None], (SSD_HEADS, LANES)))

        h0 = jnp.zeros((b, SSD_HEADS, 64, SSD_STATE), F32)
        dskip = _lane_rep(ssd_d[i], 64)
        ng = ssd_norm_g[i][None, :]
        yc_f, hc_f = _ssd(xbcc, pc, dttc, ssd_prm(0), h0, reverse=False)
        o_ssd_c, hc_b = _ssd(xbcc, pc, dttc, ssd_prm(1), h0, reverse=True, y_fwd=yc_f, dskip=dskip, ng=ng)
        y_f, _ = _ssd(xbc, p, dtt, ssd_prm(0), hc_f, reverse=False)
        o_ssd, _ = _ssd(xbc, p, dtt, ssd_prm(1), hc_b, reverse=True, y_fwd=y_f, dskip=dskip, ng=ng)

        w_out_b = w_out[i].astype(MXU_DTYPE)
        w_up_b = ffn_w_up[i].astype(MXU_DTYPE)
        w_down_b = ffn_w_down[i].astype(MXU_DTYPE)
        n2 = norm2_g[i][None, :]
        cb2 = ffn_conv_b[i][None, :]
        x = _out_proj((o_att, o_win, o_ret, o_ssd), x, g1, w_out_b)
        x = _ffn(x, n2, sc2, sh2, g2, w_up_b, ffn_conv_w[i], cb2, w_down_b,
                 final_g=None if need_ctx else final_g[None, :])
        if need_ctx:
            o_att_c = _flash(qac, ktac, vac, tq=lc, tk=lc)
            o_win_c = _flash(qwc, ktwc, vwc, sink_rows(lc), tq=lc, tk=lc)
            xc = _out_proj((o_att_c, o_win_c, o_ret_c, o_ssd_c), xc, g1c, w_out_b)
            xc = _ffn(xc, n2, sc2c, sh2c, g2c, w_up_b, ffn_conv_w[i], cb2, w_down_b)
    return x
```

```python
import functools
import math

import jax
import jax.numpy as jnp
from jax import lax
from jax.experimental import pallas as pl
from jax.experimental.pallas import tpu as pltpu

D_MODEL = 2048
GRID_W = 64
HEAD_DIM = 64
GROUP_WIDTH = D_MODEL // 4
N_HEADS = GROUP_WIDTH // HEAD_DIM
KV_HEADS = 2
Q_PER_KV = N_HEADS // KV_HEADS
KV_WIDTH = KV_HEADS * HEAD_DIM
WINDOW = 128
CHUNK = 128
SSD_HEADS = GROUP_WIDTH // 64
SSD_STATE = 128
SSD_GROUPS = 2
SSD_BC = SSD_GROUPS * SSD_STATE
SSD_CONV = 5
SSD_CONV_CH = GROUP_WIDTH + 2 * SSD_BC
FFN_DIM = ((8 * D_MODEL // 3 + 255) // 256) * 256
FFN_CONV = 3
ROPE_THETA = 10000.0
RET_THETA = 10000.0
EPS = 1e-6
NEG_INF = -1e30
LOG2E = 1.4426950408889634
IN_COLS = 8 * GROUP_WIDTH + 4 * KV_WIDTH + 2 * SSD_BC + SSD_HEADS

IN_TILE = 768
IN_COLS_PAD = 7 * IN_TILE
COL_RET = 3
COL_SZ = 7
COL_XBC = 4
COL_DT = 40

LANES = 128
HALO = 16
FFN_PARTS = 2
VT_ROWS = HEAD_DIM + 16
MXU_DTYPE = jnp.bfloat16
VMEM_LIMIT = 56 * 2 ** 20

F32 = jnp.float32


def _cparams(*sem):
    return pltpu.CompilerParams(dimension_semantics=sem, vmem_limit_bytes=VMEM_LIMIT)


def _mm(a, b):
    return jnp.dot(a.astype(MXU_DTYPE), b.astype(MXU_DTYPE), preferred_element_type=F32)


def _mm_nt(a, b):
    return lax.dot_general(a.astype(MXU_DTYPE), b.astype(MXU_DTYPE), (((1,), (1,)), ((), ())),
                           preferred_element_type=F32)


def _silu(x):
    return x * jax.nn.sigmoid(x)


def _softplus(x):
    return jnp.maximum(x, 0.0) + jnp.log1p(jnp.exp(-jnp.abs(x)))


def _log_sigmoid(x):
    return jnp.minimum(x, 0.0) - jnp.log1p(jnp.exp(-jnp.abs(x)))


def _rms(x):
    return x * lax.rsqrt(jnp.mean(x * x, axis=-1, keepdims=True) + EPS)


def _ada_kernel(c_ref, w_ref, b_ref, o_ref):
    o_ref[0] = jnp.dot(_silu(c_ref[...]), w_ref[0], preferred_element_type=F32) + b_ref[0]


def _ada(cvec, w_ada, b_ada):
    depth, d, n = w_ada.shape
    tn = 1024
    return pl.pallas_call(
        _ada_kernel,
        grid=(depth, n // tn),
        in_specs=[pl.BlockSpec((8, d), lambda l, j: (0, 0)),
                  pl.BlockSpec((1, d, tn), lambda l, j: (l, 0, j)),
                  pl.BlockSpec((1, 1, tn), lambda l, j: (l, 0, j))],
        out_specs=pl.BlockSpec((1, 8, tn), lambda l, j: (l, 0, j)),
        out_shape=jax.ShapeDtypeStruct((depth, 8, n), F32),
        compiler_params=_cparams("parallel", "parallel"),
        name="ada",
    )(cvec, w_ada, b_ada.reshape(depth, 1, n))


def _inproj_kernel(x_ref, g_ref, sc_ref, sh_ref, w_ref, o_ref, h_sc):
    @pl.when(pl.program_id(2) == 0)
    def _():
        h = _rms(x_ref[0]) * g_ref[...] * (1.0 + sc_ref[0]) + sh_ref[0]
        h_sc[...] = h.astype(h_sc.dtype)

    o_ref[0] = jnp.dot(h_sc[...], w_ref[...], preferred_element_type=F32)


def _in_proj(x, g, sc, sh, w):
    nb, lx, d = x.shape
    tm = 1024 if lx % 1024 == 0 else min(512, lx)
    return pl.pallas_call(
        _inproj_kernel,
        grid=(nb, lx // tm, IN_COLS_PAD // IN_TILE),
        in_specs=[pl.BlockSpec((1, tm, d), lambda b, i, j: (b, i, 0)),
                  pl.BlockSpec((1, d), lambda b, i, j: (0, 0)),
                  pl.BlockSpec((1, 1, d), lambda b, i, j: (b, 0, 0)),
                  pl.BlockSpec((1, 1, d), lambda b, i, j: (b, 0, 0)),
                  pl.BlockSpec((d, IN_TILE), lambda b, i, j: (0, j))],
        out_specs=pl.BlockSpec((1, tm, IN_TILE), lambda b, i, j: (b, i, j)),
        out_shape=jax.ShapeDtypeStruct((nb, lx, IN_COLS_PAD), F32),
        scratch_shapes=[pltpu.VMEM((tm, d), MXU_DTYPE)],
        compiler_params=_cparams("parallel", "parallel", "arbitrary"),
        name="in_proj",
    )(x, g, sc, sh, w)


def _rope_apply(x, cos, sa, sb, shift):
    return x * cos + pltpu.roll(x, LANES - shift, 1) * sa + pltpu.roll(x, shift, 1) * sb


def _qkv_prep_kernel(p_ref, cos_ref, sa_ref, sb_ref, gq_ref, gk_ref, bd_ref, q_ref, kt_ref, v_ref, *,
                     norm, rope, qscale, key_major):
    x = p_ref[0]

    def prep(t, g):
        if norm:
            sq = t * t
            hi = sq.astype(jnp.bfloat16)
            lo = (sq - hi.astype(F32)).astype(jnp.bfloat16)
            ss = (jnp.dot(hi, bd_ref[...], preferred_element_type=F32)
                  + jnp.dot(lo, bd_ref[...], preferred_element_type=F32))
            t = t * lax.rsqrt(ss * (1.0 / HEAD_DIM) + EPS) * g
        if rope:
            t = _rope_apply(t, cos_ref[...], sa_ref[...], sb_ref[...], HEAD_DIM // 4)
        return t

    if key_major:
        for j in range(N_HEADS // 2):
            rt = (prep(x[:, j * LANES:(j + 1) * LANES], gq_ref[...]) * qscale).T
            q_ref[0, 2 * j] = rt[:HEAD_DIM].astype(q_ref.dtype)
            q_ref[0, 2 * j + 1] = rt[HEAD_DIM:].astype(q_ref.dtype)
        k = prep(x[:, GROUP_WIDTH:GROUP_WIDTH + KV_WIDTH], gk_ref[...])
        kt_ref[0, 0] = k[:, :HEAD_DIM].astype(kt_ref.dtype)
        kt_ref[0, 1] = k[:, HEAD_DIM:].astype(kt_ref.dtype)
        vt = x[:, GROUP_WIDTH + KV_WIDTH:].T
        extra = VT_ROWS - HEAD_DIM
        tail = (lax.broadcasted_iota(jnp.int32, (extra, vt.shape[1]), 0) == 0).astype(v_ref.dtype)
        for kv in range(KV_HEADS):
            v_ref[0, kv, 0:HEAD_DIM] = vt[kv * HEAD_DIM:(kv + 1) * HEAD_DIM].astype(v_ref.dtype)
            v_ref[0, kv, HEAD_DIM:] = tail
        return
    for j in range(N_HEADS // 2):
        r = prep(x[:, j * LANES:(j + 1) * LANES], gq_ref[...]) * qscale
        q_ref[0, 2 * j] = r[:, :HEAD_DIM].astype(q_ref.dtype)
        q_ref[0, 2 * j + 1] = r[:, HEAD_DIM:].astype(q_ref.dtype)
    kt = prep(x[:, GROUP_WIDTH:GROUP_WIDTH + KV_WIDTH], gk_ref[...]).T
    kt_ref[0, 0] = kt[:HEAD_DIM].astype(kt_ref.dtype)
    kt_ref[0, 1] = kt[HEAD_DIM:].astype(kt_ref.dtype)
    v = x[:, GROUP_WIDTH + KV_WIDTH:]
    lane = lax.broadcasted_iota(jnp.int32, v.shape, 1)
    ones_col = (lane == HEAD_DIM).astype(F32)
    v_ref[0, 0] = jnp.where(lane < HEAD_DIM, v, ones_col).astype(v_ref.dtype)
    v_ref[0, 1] = jnp.where(lane < HEAD_DIM, pltpu.roll(v, HEAD_DIM, 1), ones_col).astype(v_ref.dtype)


def _qkv_prep(p, col, tabs, gq, gk, bd, *, norm, rope, qscale, key_major=False):
    nb, lx, _ = p.shape
    tl = min(256, lx)
    cos, sa, sb = tabs
    kern = functools.partial(_qkv_prep_kernel, norm=norm, rope=rope, qscale=qscale, key_major=key_major)
    tab_spec = pl.BlockSpec((tl, LANES), lambda b, i: (i, 0))
    row_spec = pl.BlockSpec((1, LANES), lambda b, i: (0, 0))
    if key_major:
        out_specs = [pl.BlockSpec((1, N_HEADS, HEAD_DIM, tl), lambda b, i: (b, 0, 0, i)),
                     pl.BlockSpec((1, KV_HEADS, tl, HEAD_DIM), lambda b, i: (b, 0, i, 0)),
                     pl.BlockSpec((1, KV_HEADS, VT_ROWS, tl), lambda b, i: (b, 0, 0, i))]
        out_shape = [jax.ShapeDtypeStruct((nb, N_HEADS, HEAD_DIM, lx), MXU_DTYPE),
                     jax.ShapeDtypeStruct((nb, KV_HEADS, lx, HEAD_DIM), MXU_DTYPE),
                     jax.ShapeDtypeStruct((nb, KV_HEADS, VT_ROWS, lx), MXU_DTYPE)]
    else:
        out_specs = [pl.BlockSpec((1, N_HEADS, tl, HEAD_DIM), lambda b, i: (b, 0, i, 0)),
                     pl.BlockSpec((1, KV_HEADS, HEAD_DIM, tl), lambda b, i: (b, 0, 0, i)),
                     pl.BlockSpec((1, KV_HEADS, tl, LANES), lambda b, i: (b, 0, i, 0))]
        out_shape = [jax.ShapeDtypeStruct((nb, N_HEADS, lx, HEAD_DIM), MXU_DTYPE),
                     jax.ShapeDtypeStruct((nb, KV_HEADS, HEAD_DIM, lx), MXU_DTYPE),
                     jax.ShapeDtypeStruct((nb, KV_HEADS, lx, LANES), MXU_DTYPE)]
    return pl.pallas_call(
        kern,
        grid=(nb, lx // tl),
        in_specs=[pl.BlockSpec((1, tl, IN_TILE), lambda b, i: (b, i, col)),
                  tab_spec, tab_spec, tab_spec, row_spec, row_spec,
                  pl.BlockSpec((LANES, LANES), lambda b, i: (0, 0))],
        out_specs=out_specs,
        out_shape=out_shape,
        compiler_params=_cparams("parallel", "parallel"),
        name="qkv_prep",
    )(p, cos, sa, sb, gq, gk, bd)


def _flash_kernel(*refs, has_sink, tq, tk, nk):
    if has_sink:
        q_ref, kt_ref, v_ref, sink_ref, o_ref, s_sc, p_sc, a_sc, m_sc, acc_sc = refs
    else:
        q_ref, kt_ref, v_ref, o_ref, s_sc, p_sc, a_sc, m_sc, acc_sc = refs
    rows = Q_PER_KV * tq
    if has_sink:
        m_sc[...] = sink_ref[0] * LOG2E
        lane = lax.broadcasted_iota(jnp.int32, acc_sc.shape, 1)
        acc_sc[...] = (lane == HEAD_DIM).astype(F32)
    else:
        m_sc[...] = jnp.full_like(m_sc, -jnp.inf)
        acc_sc[...] = jnp.zeros_like(acc_sc)
    q = q_ref[0].reshape(rows, HEAD_DIM)

    def scores(j, slot):
        off = pl.multiple_of(j * tk, tk)
        s_sc[slot] = jnp.dot(q, kt_ref[0, 0, :, pl.ds(off, tk)], preferred_element_type=F32)

    def softmax(slot):
        s = s_sc[slot]
        m_prev = m_sc[...]
        m_new = jnp.maximum(m_prev, jnp.max(s, axis=1, keepdims=True))
        a_sc[slot] = jnp.exp2(m_prev - m_new)
        p_sc[slot] = jnp.exp2(s - m_new[:, :1]).astype(p_sc.dtype)
        m_sc[...] = m_new

    def values(j, slot):
        off = pl.multiple_of(j * tk, tk)
        pv = jnp.dot(p_sc[slot], v_ref[0, 0, pl.ds(off, tk), :], preferred_element_type=F32)
        acc_sc[...] = acc_sc[...] * a_sc[slot] + pv

    def stage(j, slot, n_after):
        values(j, slot)
        if n_after >= 1:
            softmax(1 - slot)
        if n_after >= 2:
            scores(j + 2, slot)

    scores(0, 0)
    softmax(0)
    if nk > 1:
        scores(1, 1)
    pairs = max(nk - 2, 0) // 2

    def body(jj, carry):
        stage(2 * jj, 0, 2)
        stage(2 * jj + 1, 1, 2)
        return carry

    lax.fori_loop(0, pairs, body, 0)
    for j in range(2 * pairs, nk):
        stage(j, j % 2, nk - 1 - j)

    acc = acc_sc[...]
    o = acc[:, :HEAD_DIM] / acc[:, HEAD_DIM:HEAD_DIM + 1]
    for h in range(Q_PER_KV):
        o_ref[0, :, h * HEAD_DIM:(h + 1) * HEAD_DIM] = o[h * tq:(h + 1) * tq].astype(o_ref.dtype)


def _flash(q, kt, v, sink_rows=None, *, tq, tk):
    nb, _, lq, _ = q.shape
    lk = kt.shape[-1]
    rows = Q_PER_KV * tq
    has_sink = sink_rows is not None
    in_specs = [pl.BlockSpec((1, Q_PER_KV, tq, HEAD_DIM), lambda b, k, i: (b, k, i, 0)),
                pl.BlockSpec((1, 1, HEAD_DIM, lk), lambda b, k, i: (b, k, 0, 0)),
                pl.BlockSpec((1, 1, lk, LANES), lambda b, k, i: (b, k, 0, 0))]
    args = [q, kt, v]
    if has_sink:
        in_specs.append(pl.BlockSpec((1, rows, LANES), lambda b, k, i: (k, 0, 0)))
        args.append(sink_rows)
    return pl.pallas_call(
        functools.partial(_flash_kernel, has_sink=has_sink, tq=tq, tk=tk, nk=lk // tk),
        grid=(nb, KV_HEADS, lq // tq),
        in_specs=in_specs,
        out_specs=pl.BlockSpec((1, tq, Q_PER_KV * HEAD_DIM), lambda b, k, i: (b, i, k)),
        out_shape=jax.ShapeDtypeStruct((nb, lq, GROUP_WIDTH), MXU_DTYPE),
        scratch_shapes=[pltpu.VMEM((2, rows, tk), F32), pltpu.VMEM((2, rows, tk), MXU_DTYPE),
                        pltpu.VMEM((2, rows, LANES), F32), pltpu.VMEM((rows, LANES), F32),
                        pltpu.VMEM((rows, LANES), F32)],
        compiler_params=_cparams("parallel", "parallel", "parallel"),
        name="flash_attn",
    )(*args)


def _flash_km_kernel(qt_ref, k_ref, vt_ref, o_ref, s_sc, p_sc, a_sc, m_sc, acc_sc, *, tk, nk):
    m_sc[...] = jnp.full_like(m_sc, -jnp.inf)
    acc_sc[...] = jnp.zeros_like(acc_sc)

    def scores(j, slot):
        off = pl.multiple_of(j * tk, tk)
        kb = k_ref[0, 0, pl.ds(off, tk), :]
        for h in range(Q_PER_KV):
            s_sc[slot, h] = jnp.dot(kb, qt_ref[0, h], preferred_element_type=F32)

    def softmax(slot):
        for h in range(Q_PER_KV):
            s = s_sc[slot, h]
            m_prev = m_sc[h]
            m_new = jnp.maximum(m_prev, jnp.max(s, axis=0, keepdims=True))
            a_sc[slot, h] = jnp.exp2(m_prev - m_new)
            p_sc[slot, h] = jnp.exp2(s - m_new).astype(p_sc.dtype)
            m_sc[h] = m_new

    def values(j, slot):
        off = pl.multiple_of(j * tk, tk)
        vb = vt_ref[0, 0, :, pl.ds(off, tk)]
        for h in range(Q_PER_KV):
            pv = jnp.dot(vb, p_sc[slot, h], preferred_element_type=F32)
            acc_sc[h] = acc_sc[h] * a_sc[slot, h] + pv

    def stage(j, slot, n_after):
        values(j, slot)
        if n_after >= 1:
            softmax(1 - slot)
        if n_after >= 2:
            scores(j + 2, slot)

    scores(0, 0)
    softmax(0)
    if nk > 1:
        scores(1, 1)
    pairs = max(nk - 2, 0) // 2

    def body(jj, carry):
        stage(2 * jj, 0, 2)
        stage(2 * jj + 1, 1, 2)
        return carry

    lax.fori_loop(0, pairs, body, 0)
    for j in range(2 * pairs, nk):
        stage(j, j % 2, nk - 1 - j)

    for h in range(Q_PER_KV):
        acc = acc_sc[h]
        ot = acc[:HEAD_DIM] / acc[HEAD_DIM:HEAD_DIM + 1]
        o_ref[0, :, h * HEAD_DIM:(h + 1) * HEAD_DIM] = ot.T.astype(o_ref.dtype)


def _flash_km(qt, k, vt, *, tq, tk):
    nb, _, _, lq = qt.shape
    lk = k.shape[2]
    return pl.pallas_call(
        functools.partial(_flash_km_kernel, tk=tk, nk=lk // tk),
        grid=(nb, KV_HEADS, lq // tq),
        in_specs=[pl.BlockSpec((1, Q_PER_KV, HEAD_DIM, tq), lambda b, kv, i: (b, kv, 0, i)),
                  pl.BlockSpec((1, 1, lk, HEAD_DIM), lambda b, kv, i: (b, kv, 0, 0)),
                  pl.BlockSpec((1, 1, VT_ROWS, lk), lambda b, kv, i: (b, kv, 0, 0))],
        out_specs=pl.BlockSpec((1, tq, Q_PER_KV * HEAD_DIM), lambda b, kv, i: (b, i, kv)),
        out_shape=jax.ShapeDtypeStruct((nb, lq, GROUP_WIDTH), MXU_DTYPE),
        scratch_shapes=[pltpu.VMEM((2, Q_PER_KV, tk, tq), F32), pltpu.VMEM((2, Q_PER_KV, tk, tq), MXU_DTYPE),
                        pltpu.VMEM((2, Q_PER_KV, 1, tq), F32), pltpu.VMEM((Q_PER_KV, 1, tq), F32),
                        pltpu.VMEM((Q_PER_KV, VT_ROWS, tq), F32)],
        compiler_params=_cparams("parallel", "parallel", "parallel"),
        name="flash_attn_km",
    )(qt, k, vt)


def _window_km_kernel(qt_ref, kp_ref, kc_ref, kn_ref, kx_ref, vp_ref, vc_ref, vn_ref, vx_ref, sink_ref, o_ref):
    i = pl.program_id(2)
    nblk = pl.num_programs(2)
    w = WINDOW
    qb = qt_ref.shape[3]
    qt = jnp.concatenate([qt_ref[0, h] for h in range(Q_PER_KV)], axis=1)
    kb = jnp.concatenate([kp_ref[0, 0], kc_ref[0, 0], kn_ref[0, 0], kx_ref[0, 0]], axis=0)
    s = jnp.dot(kb, qt, preferred_element_type=F32)
    cols = Q_PER_KV * qb
    rel = (lax.broadcasted_iota(jnp.int32, (qb, cols), 0)
           - lax.broadcasted_iota(jnp.int32, (qb, cols), 1) % qb)
    lo = jnp.where(i > 0, 0, 2 * qb)
    hi = jnp.where(i < nblk - 1, w - qb, -2 * qb)
    sp = jnp.where(rel[:w] >= lo, s[0:w], NEG_INF)
    sc = s[w:w + qb]
    if qb > w:
        sc = jnp.where(rel >= -w, jnp.where(rel <= w, sc, NEG_INF), NEG_INF)
    sn = jnp.where(rel[:w] <= hi, s[w + qb:2 * w + qb], NEG_INF)
    sx = s[2 * w + qb:]
    sink = sink_ref[0] * LOG2E
    m = jnp.maximum(jnp.maximum(jnp.max(sp, axis=0, keepdims=True), jnp.max(sc, axis=0, keepdims=True)),
                    jnp.maximum(jnp.max(sn, axis=0, keepdims=True), jnp.max(sx, axis=0, keepdims=True)))
    m = jnp.maximum(m, sink)
    dt = vp_ref.dtype
    pt = jnp.concatenate([jnp.exp2(sp - m).astype(dt), jnp.exp2(sc - m).astype(dt),
                          jnp.exp2(sn - m).astype(dt), jnp.exp2(sx - m).astype(dt)], axis=0)
    vb = jnp.concatenate([vp_ref[0, 0], vc_ref[0, 0], vn_ref[0, 0], vx_ref[0, 0]], axis=1)
    acc = jnp.dot(vb, pt, preferred_element_type=F32)
    ot = acc[:HEAD_DIM] / (acc[HEAD_DIM:HEAD_DIM + 1] + jnp.exp2(sink - m))
    for h in range(Q_PER_KV):
        o_ref[0, :, h * HEAD_DIM:(h + 1) * HEAD_DIM] = ot[:, h * qb:(h + 1) * qb].T.astype(o_ref.dtype)


def _window_km(qt, k, vt, kx, vtx, sink_lanes, *, qb):
    nb, _, _, lq = qt.shape
    lc = kx.shape[2]
    nblk = lq // qb
    r = qb // WINDOW
    nw = lq // WINDOW

    def prev(i):
        return jnp.maximum(i * r - 1, 0)

    def nxt(i):
        return jnp.minimum((i + 1) * r, nw - 1)

    return pl.pallas_call(
        _window_km_kernel,
        grid=(nb, KV_HEADS, nblk),
        in_specs=[pl.BlockSpec((1, Q_PER_KV, HEAD_DIM, qb), lambda b, kv, i: (b, kv, 0, i)),
                  pl.BlockSpec((1, 1, WINDOW, HEAD_DIM), lambda b, kv, i: (b, kv, prev(i), 0)),
                  pl.BlockSpec((1, 1, qb, HEAD_DIM), lambda b, kv, i: (b, kv, i, 0)),
                  pl.BlockSpec((1, 1, WINDOW, HEAD_DIM), lambda b, kv, i: (b, kv, nxt(i), 0)),
                  pl.BlockSpec((1, 1, lc, HEAD_DIM), lambda b, kv, i: (b, kv, 0, 0)),
                  pl.BlockSpec((1, 1, VT_ROWS, WINDOW), lambda b, kv, i: (b, kv, 0, prev(i))),
                  pl.BlockSpec((1, 1, VT_ROWS, qb), lambda b, kv, i: (b, kv, 0, i)),
                  pl.BlockSpec((1, 1, VT_ROWS, WINDOW), lambda b, kv, i: (b, kv, 0, nxt(i))),
                  pl.BlockSpec((1, 1, VT_ROWS, lc), lambda b, kv, i: (b, kv, 0, 0)),
                  pl.BlockSpec((1, 1, Q_PER_KV * qb), lambda b, kv, i: (kv, 0, 0))],
        out_specs=pl.BlockSpec((1, qb, Q_PER_KV * HEAD_DIM), lambda b, kv, i: (b, i, kv)),
        out_shape=jax.ShapeDtypeStruct((nb, lq, GROUP_WIDTH), MXU_DTYPE),
        compiler_params=_cparams("parallel", "parallel", "parallel"),
        name="window_attn_km",
    )(qt, k, k, k, kx, vt, vt, vt, vtx, sink_lanes)


def _retention_kernel(*refs, reverse, rope, final):
    refs = list(refs)
    q_ref, k_ref, v_ref = refs[:3]
    del refs[:3]
    if rope:
        cos_ref, sa_ref, sb_ref = refs[:3]
        del refs[:3]
    lg_ref, lgw_ref, s0_ref = refs[:3]
    del refs[:3]
    if final:
        of_ref, gate_ref, gn_ref = refs[:3]
        del refs[:3]
    o_ref, sfin_ref, s_sc, d_sc = refs
    c = CHUNK
    n = pl.program_id(0)
    nb = q_ref.shape[0]
    pairs = N_HEADS // 2
    lane = lax.broadcasted_iota(jnp.int32, (c, LANES), 1)
    first = lane < HEAD_DIM
    blk = (lax.broadcasted_iota(jnp.int32, (LANES, LANES), 0) < HEAD_DIM) == (
        lax.broadcasted_iota(jnp.int32, (LANES, LANES), 1) < HEAD_DIM)

    @pl.when(n == 0)
    def _():
        for b in range(nb):
            for t in range(pairs):
                s_sc[b, t] = jnp.zeros((LANES, LANES), F32)
                s_sc[b, t, 0:HEAD_DIM, 0:HEAD_DIM] = s0_ref[b, 2 * t]
                s_sc[b, t, HEAD_DIM:, HEAD_DIM:] = s0_ref[b, 2 * t + 1]
        ri = lax.broadcasted_iota(jnp.int32, (c, c), 0)
        ci = lax.broadcasted_iota(jnp.int32, (c, c), 1)
        rel = (ci - ri if reverse else ri - ci).astype(F32)
        for h in range(N_HEADS):
            ld = _log_sigmoid(lg_ref[h])
            d_sc[h] = jnp.where(rel >= 0, jnp.exp(jnp.maximum(rel, 0.0) * ld), 0.0)

    ldw = _log_sigmoid(lgw_ref[...])
    idx = lax.broadcasted_iota(jnp.int32, (c, GROUP_WIDTH), 0).astype(F32)
    if reverse:
        q_scale = jnp.exp((c - idx) * ldw)
        k_scale = jnp.exp(idx * ldw)
    else:
        q_scale = jnp.exp((idx + 1.0) * ldw)
        k_scale = jnp.exp((c - 1.0 - idx) * ldw)
    chunk_decay = jnp.exp(c * ldw)

    for b in range(nb):
        q, k, v = q_ref[b], k_ref[b], v_ref[b]
        outs = []
        for t in range(pairs):
            sl = slice(t * LANES, (t + 1) * LANES)
            qp, kp, vp = q[:, sl], k[:, sl], v[:, sl]
            if rope:
                qp = _rope_apply(qp, cos_ref[...], sa_ref[...], sb_ref[...], HEAD_DIM // 2)
                kp = _rope_apply(kp, cos_ref[...], sa_ref[...], sb_ref[...], HEAD_DIM // 2)
            kp = kp * (HEAD_DIM ** -0.5)
            kb, vb = kp.astype(MXU_DTYPE), vp.astype(MXU_DTYPE)
            s0 = _mm_nt(jnp.where(first, qp, 0.0), kb) * d_sc[2 * t]
            s1 = _mm_nt(jnp.where(first, 0.0, qp), kb) * d_sc[2 * t + 1]
            o = jnp.where(first, _mm(s0, vb), _mm(s1, vb))
            o = o + _mm(qp * q_scale[:, sl], s_sc[b, t])
            kv = lax.dot_general((kp * k_scale[:, sl]).astype(MXU_DTYPE), vb,
                                 (((0,), (0,)), ((), ())), preferred_element_type=F32)
            s_sc[b, t] = s_sc[b, t] * chunk_decay[:, sl] + jnp.where(blk, kv, 0.0)
            if final:
                o = o + of_ref[b][:, sl]
                inv = 1.0 / HEAD_DIM
                mu0 = jnp.sum(jnp.where(first, o, 0.0), axis=1, keepdims=True) * inv
                mu1 = jnp.sum(jnp.where(first, 0.0, o), axis=1, keepdims=True) * inv
                dlt = o - jnp.where(first, mu0, mu1)
                sq = dlt * dlt
                var0 = jnp.sum(jnp.where(first, sq, 0.0), axis=1, keepdims=True) * inv
                var1 = jnp.sum(jnp.where(first, 0.0, sq), axis=1, keepdims=True) * inv
                o = dlt * lax.rsqrt(jnp.where(first, var0, var1) + EPS)
            outs.append(o)
        o = jnp.concatenate(outs, axis=1)
        if final:
            o = o * gn_ref[...] * _silu(gate_ref[b])
        o_ref[b] = o.astype(o_ref.dtype)

    @pl.when(n == pl.num_programs(0) - 1)
    def _():
        for b in range(nb):
            for t in range(pairs):
                sfin_ref[b, 2 * t] = s_sc[b, t, 0:HEAD_DIM, 0:HEAD_DIM]
                sfin_ref[b, 2 * t + 1] = s_sc[b, t, HEAD_DIM:, HEAD_DIM:]


def _retention(p, tabs, lg, lgw, s0, *, reverse, o_fwd=None, gn=None):
    nb, lx, _ = p.shape
    c = CHUNK
    nc = lx // c
    rope = tabs is not None
    final = o_fwd is not None

    def cm(n):
        return nc - 1 - n if reverse else n

    def col(j):
        return pl.BlockSpec((nb, c, GROUP_WIDTH), lambda n: (0, cm(n), j))

    state = pl.BlockSpec((nb, N_HEADS, HEAD_DIM, HEAD_DIM), lambda n: (0, 0, 0, 0))
    in_specs = [col(COL_RET), col(COL_RET + 1), col(COL_RET + 2)]
    args = [p, p, p]
    if rope:
        in_specs += [pl.BlockSpec((c, LANES), lambda n: (cm(n), 0))] * 3
        args += list(tabs)
    in_specs += [pl.BlockSpec((N_HEADS, 1, LANES), lambda n: (0, 0, 0)),
                 pl.BlockSpec((1, GROUP_WIDTH), lambda n: (0, 0)), state]
    args += [lg, lgw, s0]
    if final:
        in_specs += [col(0), col(COL_RET + 3), pl.BlockSpec((1, GROUP_WIDTH), lambda n: (0, 0))]
        args += [o_fwd, p, gn]
    return pl.pallas_call(
        functools.partial(_retention_kernel, reverse=reverse, rope=rope, final=final),
        grid=(nc,),
        in_specs=in_specs,
        out_specs=[col(0), state],
        out_shape=[jax.ShapeDtypeStruct((nb, lx, GROUP_WIDTH), MXU_DTYPE if final else F32),
                   jax.ShapeDtypeStruct((nb, N_HEADS, HEAD_DIM, HEAD_DIM), F32)],
        scratch_shapes=[pltpu.VMEM((nb, N_HEADS // 2, LANES, LANES), F32), pltpu.VMEM((N_HEADS, c, c), F32)],
        compiler_params=_cparams("arbitrary"),
        name="retention",
    )(*args)


def _ssd_conv_kernel(xp_ref, xc_ref, xn_ref, w_ref, b_ref, o_ref, x_sc):
    i = pl.program_id(1)
    tl = xc_ref.shape[1]
    x_sc[0:HALO] = jnp.where(i > 0, xp_ref[0], 0.0)
    x_sc[HALO:HALO + tl] = xc_ref[0]
    x_sc[HALO + tl:] = jnp.where(i < pl.num_programs(1) - 1, xn_ref[0], 0.0)
    acc = jnp.zeros((tl, xc_ref.shape[2]), F32) + b_ref[...]
    for k in range(SSD_CONV):
        off = HALO + k - SSD_CONV // 2
        acc = acc + x_sc[off:off + tl] * w_ref[k:k + 1]
    o_ref[0] = _silu(acc)


def _ssd_conv(p, w, b):
    nb, lx, _ = p.shape
    tl = min(256, lx)
    nh = lx // HALO
    r = tl // HALO
    ch = SSD_CONV_CH
    col = COL_XBC
    return pl.pallas_call(
        _ssd_conv_kernel,
        grid=(nb, lx // tl),
        in_specs=[pl.BlockSpec((1, HALO, ch), lambda b_, i: (b_, jnp.maximum(i * r - 1, 0), col)),
                  pl.BlockSpec((1, tl, ch), lambda b_, i: (b_, i, col)),
                  pl.BlockSpec((1, HALO, ch), lambda b_, i: (b_, jnp.minimum((i + 1) * r, nh - 1), col)),
                  pl.BlockSpec((SSD_CONV, ch), lambda b_, i: (0, 0)),
                  pl.BlockSpec((1, ch), lambda b_, i: (0, 0))],
        out_specs=pl.BlockSpec((1, tl, ch), lambda b_, i: (b_, i, 0)),
        out_shape=jax.ShapeDtypeStruct((nb, lx, ch), F32),
        scratch_shapes=[pltpu.VMEM((tl + 2 * HALO, ch), F32)],
        compiler_params=_cparams("parallel", "parallel"),
        name="ssd_conv",
    )(p, p, p, w, b)


def _ssd_kernel(*refs, reverse, final):
    refs = list(refs)
    xbc_ref, dt_ref, dtt_ref, bias_row_ref, alog_row_ref, bias_col_ref, alog_col_ref, h0_ref = refs[:8]
    del refs[:8]
    if final:
        yf_ref, z_ref, dskip_ref, ng_ref = refs[:4]
        del refs[:4]
    y_ref, hfin_ref, h_sc = refs
    c = CHUNK
    n = pl.program_id(0)
    nb = xbc_ref.shape[0]
    hp = lax.Precision.HIGHEST

    @pl.when(n == 0)
    def _():
        h_sc[...] = h0_ref[...]

    ri = lax.broadcasted_iota(jnp.int32, (c, c), 0)
    ci = lax.broadcasted_iota(jnp.int32, (c, c), 1)
    tri_b = (ri <= ci) if reverse else (ri >= ci)
    tri = tri_b.astype(F32)
    ones = jnp.ones((c, c), F32)

    for b in range(nb):
        dt = _softplus(dt_ref[b] + bias_row_ref[...])
        a = dt * (-jnp.exp(alog_row_ref[...]))
        a_cum = jnp.dot(tri, a, precision=hp, preferred_element_type=F32)
        dtt = _softplus(dtt_ref[b] + bias_col_ref[...])
        at = dtt * (-jnp.exp(alog_col_ref[...]))
        a_cum_t = lax.dot_general(at, tri, (((1,), (1,)), ((), ())), precision=hp, preferred_element_type=F32)
        a_tot_t = jnp.dot(at, ones, precision=hp, preferred_element_type=F32)

        xbc = xbc_ref[b]
        x = xbc[:, :GROUP_WIDTH]
        xt = x.T
        outs = []
        for g in range(SSD_GROUPS):
            bm = xbc[:, GROUP_WIDTH + g * SSD_STATE:GROUP_WIDTH + (g + 1) * SSD_STATE]
            cmat = xbc[:, GROUP_WIDTH + SSD_BC + g * SSD_STATE:GROUP_WIDTH + SSD_BC + (g + 1) * SSD_STATE]
            cb = _mm_nt(cmat, bm)
            for hh in range(SSD_HEADS // SSD_GROUPS):
                h = g * (SSD_HEADS // SSD_GROUPS) + hh
                sl = slice(h * 64, (h + 1) * 64)
                col = a_cum[:, h:h + 1]
                row = a_cum_t[h:h + 1, :]
                dt_row = dtt[h:h + 1, :]
                tot_row = a_tot_t[h:h + 1, :]
                decay = jnp.exp(jnp.where(tri_b, col - row, NEG_INF))
                w = cb * decay * dt_row
                y = _mm(w, x[:, sl]) + _mm_nt(cmat, h_sc[b, h]) * jnp.exp(col)
                to_end = jnp.exp(tot_row - row) * dt_row
                st = _mm(xt[sl, :] * to_end, bm)
                h_sc[b, h] = h_sc[b, h] * jnp.exp(tot_row) + st
                outs.append(y)
        y = jnp.concatenate(outs, axis=1)
        if final:
            y = (yf_ref[b] + y + dskip_ref[...] * x) * _silu(z_ref[b])
            half = GROUP_WIDTH // SSD_GROUPS
            y = jnp.concatenate([_rms(y[:, :half]), _rms(y[:, half:])], axis=1) * ng_ref[...]
        y_ref[b] = y.astype(y_ref.dtype)

    @pl.when(n == pl.num_programs(0) - 1)
    def _():
        hfin_ref[...] = h_sc[...]


def _ssd(xbc, p, dtt, prm, h0, *, reverse, y_fwd=None, dskip=None, ng=None):
    nb, lx, _ = xbc.shape
    c = CHUNK
    nc = lx // c
    final = y_fwd is not None

    def cm(n):
        return nc - 1 - n if reverse else n

    row = pl.BlockSpec((1, LANES), lambda n: (0, 0))
    colb = pl.BlockSpec((SSD_HEADS, LANES), lambda n: (0, 0))
    wide = pl.BlockSpec((1, GROUP_WIDTH), lambda n: (0, 0))
    state = pl.BlockSpec((nb, SSD_HEADS, 64, SSD_STATE), lambda n: (0, 0, 0, 0))
    yspec = pl.BlockSpec((nb, c, GROUP_WIDTH), lambda n: (0, cm(n), 0))
    in_specs = [pl.BlockSpec((nb, c, SSD_CONV_CH), lambda n: (0, cm(n), 0)),
                pl.BlockSpec((nb, c, LANES), lambda n: (0, cm(n), COL_DT)),
                pl.BlockSpec((nb, SSD_HEADS, c), lambda n: (0, 0, cm(n))),
                row, row, colb, colb, state]
    args = [xbc, p, dtt, *prm, h0]
    if final:
        in_specs += [yspec, pl.BlockSpec((nb, c, GROUP_WIDTH), lambda n: (0, cm(n), COL_SZ)), wide, wide]
        args += [y_fwd, p, dskip, ng]
    return pl.pallas_call(
        functools.partial(_ssd_kernel, reverse=reverse, final=final),
        grid=(nc,),
        in_specs=in_specs,
        out_specs=[yspec, state],
        out_shape=[jax.ShapeDtypeStruct((nb, lx, GROUP_WIDTH), MXU_DTYPE if final else F32),
                   jax.ShapeDtypeStruct((nb, SSD_HEADS, 64, SSD_STATE), F32)],
        scratch_shapes=[pltpu.VMEM((nb, SSD_HEADS, 64, SSD_STATE), F32)],
        compiler_params=_cparams("arbitrary"),
        name="ssd_scan",
    )(*args)


def _outproj_kernel(a_ref, w_ref_in, r_ref, s_ref, x_ref, g_ref, w_ref, o_ref):
    acc = jnp.zeros(o_ref.shape[1:], F32)
    for m, ref in enumerate((a_ref, w_ref_in, r_ref, s_ref)):
        acc = acc + jnp.dot(ref[0], w_ref[m * GROUP_WIDTH:(m + 1) * GROUP_WIDTH, :], preferred_element_type=F32)
    o_ref[0] = x_ref[0] + g_ref[0] * acc


def _out_proj(mix, x, g1, w):
    nb, lx, d = x.shape
    tm = min(512, lx)
    mspec = pl.BlockSpec((1, tm, GROUP_WIDTH), lambda b, i: (b, i, 0))
    return pl.pallas_call(
        _outproj_kernel,
        grid=(nb, lx // tm),
        in_specs=[mspec, mspec, mspec, mspec,
                  pl.BlockSpec((1, tm, d), lambda b, i: (b, i, 0)),
                  pl.BlockSpec((1, 1, d), lambda b, i: (b, 0, 0)),
                  pl.BlockSpec((4 * GROUP_WIDTH, d), lambda b, i: (0, 0))],
        out_specs=pl.BlockSpec((1, tm, d), lambda b, i: (b, i, 0)),
        out_shape=jax.ShapeDtypeStruct((nb, lx, d), F32),
        compiler_params=_cparams("parallel", "parallel"),
        name="out_proj",
    )(*mix, x, g1, w)


def _ffn_kernel(*refs, final):
    refs = list(refs)
    (xp_ref, xc_ref, xn_ref, g_ref, sc_ref, sh_ref, gate_ref, wg_ref, wv_ref, cwg_ref, cwv_ref, cbg_ref, cbv_ref,
     wd_ref) = refs[:14]
    del refs[:14]
    if final:
        fg_ref = refs.pop(0)
    o_ref, h_sc, acc_sc, ug_sc, uv_sc = refs
    i, j = pl.program_id(1), pl.program_id(2)
    tm = xc_ref.shape[1]

    @pl.when(j == 0)
    def _():
        def nm(t):
            return (_rms(t) * g_ref[...] * (1.0 + sc_ref[0]) + sh_ref[0]).astype(h_sc.dtype)
        h_sc[0:HALO] = nm(xp_ref[0])
        h_sc[HALO:HALO + tm] = nm(xc_ref[0])
        h_sc[HALO + tm:] = nm(xn_ref[0])
        acc_sc[...] = jnp.zeros_like(acc_sc)

    h = h_sc[...]
    keep_prev = jnp.where(i > 0, 1.0, 0.0)
    keep_next = jnp.where(i < pl.num_programs(1) - 1, 1.0, 0.0)
    parts = ug_sc.shape[0]
    th = ug_sc.shape[2]

    def conv(u_ref, w_ref, b_ref, cs):
        acc = jnp.zeros((tm, th), F32) + b_ref[:, cs]
        for k in range(FFN_CONV):
            off = HALO + k - FFN_CONV // 2
            acc = acc + u_ref[off:off + tm] * w_ref[k:k + 1, cs]
        return acc

    down = None
    for a in range(parts):
        cs = slice(a * th, (a + 1) * th)
        for w_ref, u_sc in ((wg_ref, ug_sc), (wv_ref, uv_sc)):
            u_sc[a] = jnp.dot(h, w_ref[:, cs], preferred_element_type=F32)
            u_sc[a, 0:HALO] = u_sc[a, 0:HALO] * keep_prev
            u_sc[a, HALO + tm:] = u_sc[a, HALO + tm:] * keep_next
    for a in range(parts):
        cs = slice(a * th, (a + 1) * th)
        act = _silu(conv(ug_sc.at[a], cwg_ref, cbg_ref, cs)) * conv(uv_sc.at[a], cwv_ref, cbv_ref, cs)
        d = jnp.dot(act.astype(wd_ref.dtype), wd_ref[cs, :], preferred_element_type=F32)
        down = d if down is None else down + d
    acc_sc[...] += down

    @pl.when(j == pl.num_programs(2) - 1)
    def _():
        y = xc_ref[0] + gate_ref[0] * acc_sc[...]
        if final:
            y = _rms(y) * fg_ref[...]
        o_ref[0] = y


def _ffn(x, g, sc, sh, gate, w_up, conv_w, conv_b, w_down, final_g=None):
    nb, lx, d = x.shape
    f = w_down.shape[0]
    tm = min(512, lx)
    tf = 512
    nf = f // tf
    nh = lx // HALO
    r = tm // HALO
    final = final_g is not None
    vec = pl.BlockSpec((1, d), lambda b, i, j: (0, 0))
    mod = pl.BlockSpec((1, 1, d), lambda b, i, j: (b, 0, 0))
    in_specs = [pl.BlockSpec((1, HALO, d), lambda b, i, j: (b, jnp.maximum(i * r - 1, 0), 0)),
                pl.BlockSpec((1, tm, d), lambda b, i, j: (b, i, 0)),
                pl.BlockSpec((1, HALO, d), lambda b, i, j: (b, jnp.minimum((i + 1) * r, nh - 1), 0)),
                vec, mod, mod, mod,
                pl.BlockSpec((d, tf), lambda b, i, j: (0, j)),
                pl.BlockSpec((d, tf), lambda b, i, j: (0, nf + j)),
                pl.BlockSpec((FFN_CONV, tf), lambda b, i, j: (0, j)),
                pl.BlockSpec((FFN_CONV, tf), lambda b, i, j: (0, nf + j)),
                pl.BlockSpec((1, tf), lambda b, i, j: (0, j)),
                pl.BlockSpec((1, tf), lambda b, i, j: (0, nf + j)),
                pl.BlockSpec((tf, d), lambda b, i, j: (j, 0))]
    args = [x, x, x, g, sc, sh, gate, w_up, w_up, conv_w, conv_w, conv_b, conv_b, w_down]
    if final:
        in_specs.append(vec)
        args.append(final_g)
    return pl.pallas_call(
        functools.partial(_ffn_kernel, final=final),
        grid=(nb, lx // tm, nf),
        in_specs=in_specs,
        out_specs=pl.BlockSpec((1, tm, d), lambda b, i, j: (b, i, 0)),
        out_shape=jax.ShapeDtypeStruct((nb, lx, d), F32),
        scratch_shapes=[pltpu.VMEM((tm + 2 * HALO, d), MXU_DTYPE), pltpu.VMEM((tm, d), F32),
                        pltpu.VMEM((FFN_PARTS, tm + 2 * HALO, tf // FFN_PARTS), F32),
                        pltpu.VMEM((FFN_PARTS, tm + 2 * HALO, tf // FFN_PARTS), F32)],
        compiler_params=_cparams("parallel", "parallel", "arbitrary"),
        name="conv_ffn",
    )(*args)


def _pair_tables(ang_blocks):
    cos, sa, sb = [], [], []
    for ang in ang_blocks:
        c, s = jnp.cos(ang), jnp.sin(ang)
        z = jnp.zeros_like(s)
        cos += [c, c]
        sa += [-s, z]
        sb += [z, s]
    def two(parts):
        t = jnp.concatenate(parts, axis=-1)
        return jnp.concatenate([t, t], axis=-1)
    return two(cos), two(sa), two(sb)


def _lane_rep(v, width):
    return jnp.repeat(v, width)[None, :]


def kernel(x, c, ctx, c_ctx, w_ada, b_ada, norm1_g, w_in, attn_qn_g, attn_kn_g, win_sink, ret_decay_logit,
           ret_gn_g, ssd_conv_w, ssd_conv_b, ssd_a_log, ssd_dt_bias, ssd_d, ssd_norm_g, w_out, norm2_g,
           ffn_w_up, ffn_conv_w, ffn_conv_b, ffn_w_down, final_g):
    b, l, d = x.shape
    lc = ctx.shape[1]
    depth = w_ada.shape[0]
    assert d == D_MODEL and l % 512 == 0 and lc % 256 == 0 and b + 1 <= 8

    rows = l // GRID_W
    row = jnp.repeat(jnp.arange(rows), GRID_W).astype(F32)
    colp = jnp.tile(jnp.arange(GRID_W), rows).astype(F32)
    n_ax = HEAD_DIM // 4
    inv_ax = ROPE_THETA ** (-jnp.arange(n_ax, dtype=F32) / n_ax)
    tabs_2d = _pair_tables([row[:, None] * inv_ax, colp[:, None] * inv_ax])
    n_ret = HEAD_DIM // 2
    inv_ret = RET_THETA ** (-jnp.arange(n_ret, dtype=F32) / n_ret)
    tabs_ret = _pair_tables([jnp.arange(l, dtype=F32)[:, None] * inv_ret])
    tabs_ctx = (jnp.ones((lc, LANES), F32), jnp.zeros((lc, LANES), F32), jnp.zeros((lc, LANES), F32))
    bd = (jnp.arange(LANES)[:, None] // HEAD_DIM == jnp.arange(LANES)[None, :] // HEAD_DIM).astype(jnp.bfloat16)

    cvec = jnp.zeros((8, d), F32).at[:b].set(c).at[b].set(c_ctx)
    mod_all = _ada(cvec, w_ada, b_ada)

    qscale = HEAD_DIM ** -0.5 * LOG2E
    xc = ctx
    for i in range(depth):
        need_ctx = i < depth - 1
        mod = mod_all[i].reshape(8, 6, 1, d)
        sh1, sc1, g1, sh2, sc2, g2 = [mod[:b, j] for j in range(6)]
        sh1c, sc1c, g1c, sh2c, sc2c, g2c = [jnp.broadcast_to(mod[b:b + 1, j], (b, 1, d)) for j in range(6)]

        w_in_p = jnp.pad(w_in[i], ((0, 0), (0, IN_COLS_PAD - IN_COLS))).astype(MXU_DTYPE)
        n1 = norm1_g[i][None, :]
        p = _in_proj(x, n1, sc1, sh1, w_in_p)
        pc = _in_proj(xc, n1, sc1c, sh1c, w_in_p)

        gq = jnp.tile(attn_qn_g[i], 2)[None, :]
        gk = jnp.tile(attn_kn_g[i], 2)[None, :]
        qat, ka, vat = _qkv_prep(p, 0, tabs_2d, gq, gk, bd, norm=True, rope=True, qscale=qscale, key_major=True)
        _, kac, vatc = _qkv_prep(pc, 0, tabs_ctx, gq, gk, bd, norm=True, rope=False, qscale=qscale,
                                 key_major=True)
        if need_ctx:
            qac, ktac, vac = _qkv_prep(pc, 0, tabs_ctx, gq, gk, bd, norm=True, rope=False, qscale=qscale)
        qwt, kw, vwt = _qkv_prep(p, 1, tabs_2d, gq, gk, bd, norm=False, rope=True, qscale=qscale, key_major=True)
        _, kwc, vwtc = _qkv_prep(pc, 1, tabs_ctx, gq, gk, bd, norm=False, rope=False, qscale=qscale,
                                 key_major=True)
        if need_ctx:
            qwc, ktwc, vwc = _qkv_prep(pc, 1, tabs_ctx, gq, gk, bd, norm=False, rope=False, qscale=qscale)
        sink = win_sink[i].reshape(KV_HEADS, Q_PER_KV)

        def sink_rows(t):
            return jnp.broadcast_to(sink[:, :, None, None], (KV_HEADS, Q_PER_KV, t, LANES)).reshape(
                KV_HEADS, Q_PER_KV * t, LANES)

        tk = (l + lc) // 13 if (l + lc) % (13 * 128) == 0 else 128
        o_att = _flash_km(qat, jnp.concatenate([ka, kac], axis=2), jnp.concatenate([vat, vatc], axis=3),
                          tq=256, tk=tk)
        win_qb = 2 * WINDOW
        sink_lanes = jnp.repeat(sink, win_qb, axis=1)[:, None, :]
        o_win = _window_km(qwt, kw, vwt, kwc, vwtc, sink_lanes, qb=win_qb)

        lg = jnp.broadcast_to(ret_decay_logit[i][:, :, None, None], (2, N_HEADS, 1, LANES))
        lgw = [_lane_rep(ret_decay_logit[i][dr], HEAD_DIM) for dr in range(2)]
        s0 = jnp.zeros((b, N_HEADS, HEAD_DIM, HEAD_DIM), F32)
        gn = ret_gn_g[i][None, :]
        oc_f, sc_f = _retention(pc, None, lg[0], lgw[0], s0, reverse=False)
        o_ret_c, sc_b = _retention(pc, None, lg[1], lgw[1], s0, reverse=True, o_fwd=oc_f, gn=gn)
        o_f, _ = _retention(p, tabs_ret, lg[0], lgw[0], sc_f, reverse=False)
        o_ret, _ = _retention(p, tabs_ret, lg[1], lgw[1], sc_b, reverse=True, o_fwd=o_f, gn=gn)

        conv_b = ssd_conv_b[i][None, :]
        xbc = _ssd_conv(p, ssd_conv_w[i], conv_b)
        xbcc = _ssd_conv(pc, ssd_conv_w[i], conv_b)
        dtt = jnp.swapaxes(p[:, :, COL_DT * LANES:COL_DT * LANES + SSD_HEADS], 1, 2)
        dttc = jnp.swapaxes(pc[:, :, COL_DT * LANES:COL_DT * LANES + SSD_HEADS], 1, 2)

        def ssd_prm(dr):
            pad = jnp.zeros((LANES - SSD_HEADS,), F32)
            return (jnp.concatenate([ssd_dt_bias[i][dr], pad])[None, :],
                    jnp.concatenate([ssd_a_log[i][dr], pad])[None, :],
                    jnp.broadcast_to(ssd_dt_bias[i][dr][:, None], (SSD_HEADS, LANES)),
                    jnp.broadcast_to(ssd_a_log[i][dr][:, None], (SSD_HEADS, LANES)))

        h0 = jnp.zeros((b, SSD_HEADS, 64, SSD_STATE), F32)
        dskip = _lane_rep(ssd_d[i], 64)
        ng = ssd_norm_g[i][None, :]
        yc_f, hc_f = _ssd(xbcc, pc, dttc, ssd_prm(0), h0, reverse=False)
        o_ssd_c, hc_b = _ssd(xbcc, pc, dttc, ssd_prm(1), h0, reverse=True, y_fwd=yc_f, dskip=dskip, ng=ng)
        y_f, _ = _ssd(xbc, p, dtt, ssd_prm(0), hc_f, reverse=False)
        o_ssd, _ = _ssd(xbc, p, dtt, ssd_prm(1), hc_b, reverse=True, y_fwd=y_f, dskip=dskip, ng=ng)

        w_out_b = w_out[i].astype(MXU_DTYPE)
        w_up_b = ffn_w_up[i].astype(MXU_DTYPE)
        w_down_b = ffn_w_down[i].astype(MXU_DTYPE)
        n2 = norm2_g[i][None, :]
        cb2 = ffn_conv_b[i][None, :]
        x = _out_proj((o_att, o_win, o_ret, o_ssd), x, g1, w_out_b)
        x = _ffn(x, n2, sc2, sh2, g2, w_up_b, ffn_conv_w[i], cb2, w_down_b,
                 final_g=None if need_ctx else final_g[None, :])
        if need_ctx:
            o_att_c = _flash(qac, ktac, vac, tq=lc, tk=lc)
            o_win_c = _flash(qwc, ktwc, vwc, sink_rows(lc), tq=lc, tk=lc)
            xc = _out_proj((o_att_c, o_win_c, o_ret_c, o_ssd_c), xc, g1c, w_out_b)
            xc = _ffn(xc, n2, sc2c, sh2c, g2c, w_up_b, ffn_conv_w[i], cb2, w_down_b)
    return x
```

```python
import functools
import math

import jax
import jax.numpy as jnp
from jax import lax
from jax.experimental import pallas as pl
from jax.experimental.pallas import tpu as pltpu

D_MODEL = 2048
GRID_W = 64
HEAD_DIM = 64
GROUP_WIDTH = D_MODEL // 4
N_HEADS = GROUP_WIDTH // HEAD_DIM
KV_HEADS = 2
Q_PER_KV = N_HEADS // KV_HEADS
KV_WIDTH = KV_HEADS * HEAD_DIM
WINDOW = 128
CHUNK = 128
SSD_HEADS = GROUP_WIDTH // 64
SSD_STATE = 128
SSD_GROUPS = 2
SSD_BC = SSD_GROUPS * SSD_STATE
SSD_CONV = 5
SSD_CONV_CH = GROUP_WIDTH + 2 * SSD_BC
FFN_DIM = ((8 * D_MODEL // 3 + 255) // 256) * 256
FFN_CONV = 3
ROPE_THETA = 10000.0
RET_THETA = 10000.0
EPS = 1e-6
NEG_INF = -1e30
LOG2E = 1.4426950408889634
IN_COLS = 8 * GROUP_WIDTH + 4 * KV_WIDTH + 2 * SSD_BC + SSD_HEADS

IN_TILE = 768
IN_COLS_PAD = 7 * IN_TILE
COL_RET = 3
COL_SZ = 7
COL_XBC = 4
COL_DT = 40

LANES = 128
HALO = 16
FFN_PARTS = 2
VT_ROWS = HEAD_DIM + 16

ADA_TN = 1024
PROJ_TM = 1024
TOKEN_TM = 512
FFN_TF = 512
ROW_TILE = 256
FLASH_TQ = 128
FLASH_KEY_BLOCKS = 13
WIN_QB = 2 * WINDOW
MXU_DTYPE = jnp.bfloat16
VMEM_LIMIT = 56 * 2 ** 20

F32 = jnp.float32


def _cparams(*sem):
    return pltpu.CompilerParams(dimension_semantics=sem, vmem_limit_bytes=VMEM_LIMIT)


def _mm(a, b):
    return jnp.dot(a.astype(MXU_DTYPE), b.astype(MXU_DTYPE), preferred_element_type=F32)


def _mm_nt(a, b):
    return lax.dot_general(a.astype(MXU_DTYPE), b.astype(MXU_DTYPE), (((1,), (1,)), ((), ())),
                           preferred_element_type=F32)


def _silu(x):
    return x * jax.nn.sigmoid(x)


def _softplus(x):
    return jnp.maximum(x, 0.0) + jnp.log1p(jnp.exp(-jnp.abs(x)))


def _log_sigmoid(x):
    return jnp.minimum(x, 0.0) - jnp.log1p(jnp.exp(-jnp.abs(x)))


def _rms(x):
    return x * lax.rsqrt(jnp.mean(x * x, axis=-1, keepdims=True) + EPS)


def _ada_kernel(c_ref, w_ref, b_ref, o_ref):
    o_ref[0] = jnp.dot(_silu(c_ref[...]), w_ref[0], preferred_element_type=F32) + b_ref[0]


def _ada(cvec, w_ada, b_ada):
    depth, d, n = w_ada.shape
    tn = ADA_TN
    return pl.pallas_call(
        _ada_kernel,
        grid=(depth, n // tn),
        in_specs=[pl.BlockSpec((8, d), lambda l, j: (0, 0)),
                  pl.BlockSpec((1, d, tn), lambda l, j: (l, 0, j)),
                  pl.BlockSpec((1, 1, tn), lambda l, j: (l, 0, j))],
        out_specs=pl.BlockSpec((1, 8, tn), lambda l, j: (l, 0, j)),
        out_shape=jax.ShapeDtypeStruct((depth, 8, n), F32),
        compiler_params=_cparams("parallel", "parallel"),
        name="ada",
    )(cvec, w_ada, b_ada.reshape(depth, 1, n))


def _inproj_kernel(x_ref, g_ref, sc_ref, sh_ref, w_ref, o_ref, h_sc):
    @pl.when(pl.program_id(2) == 0)
    def _():
        h = _rms(x_ref[0]) * g_ref[...] * (1.0 + sc_ref[0]) + sh_ref[0]
        h_sc[...] = h.astype(h_sc.dtype)

    o_ref[0] = jnp.dot(h_sc[...], w_ref[...], preferred_element_type=F32)


def _in_proj(x, g, sc, sh, w):
    nb, lx, d = x.shape
    tm = PROJ_TM if lx % PROJ_TM == 0 else min(TOKEN_TM, lx)
    return pl.pallas_call(
        _inproj_kernel,
        grid=(nb, lx // tm, IN_COLS_PAD // IN_TILE),
        in_specs=[pl.BlockSpec((1, tm, d), lambda b, i, j: (b, i, 0)),
                  pl.BlockSpec((1, d), lambda b, i, j: (0, 0)),
                  pl.BlockSpec((1, 1, d), lambda b, i, j: (b, 0, 0)),
                  pl.BlockSpec((1, 1, d), lambda b, i, j: (b, 0, 0)),
                  pl.BlockSpec((d, IN_TILE), lambda b, i, j: (0, j))],
        out_specs=pl.BlockSpec((1, tm, IN_TILE), lambda b, i, j: (b, i, j)),
        out_shape=jax.ShapeDtypeStruct((nb, lx, IN_COLS_PAD), F32),
        scratch_shapes=[pltpu.VMEM((tm, d), MXU_DTYPE)],
        compiler_params=_cparams("parallel", "parallel", "arbitrary"),
        name="in_proj",
    )(x, g, sc, sh, w)


def _rope_apply(x, cos, sa, sb, shift):
    return x * cos + pltpu.roll(x, LANES - shift, 1) * sa + pltpu.roll(x, shift, 1) * sb


def _qkv_prep_kernel(p_ref, cos_ref, sa_ref, sb_ref, gq_ref, gk_ref, bd_ref, q_ref, kt_ref, v_ref, *,
                     norm, rope, qscale, key_major):
    x = p_ref[0]

    def prep(t, g):
        if norm:
            sq = t * t
            hi = sq.astype(jnp.bfloat16)
            lo = (sq - hi.astype(F32)).astype(jnp.bfloat16)
            ss = (jnp.dot(hi, bd_ref[...], preferred_element_type=F32)
                  + jnp.dot(lo, bd_ref[...], preferred_element_type=F32))
            t = t * lax.rsqrt(ss * (1.0 / HEAD_DIM) + EPS) * g
        if rope:
            t = _rope_apply(t, cos_ref[...], sa_ref[...], sb_ref[...], HEAD_DIM // 4)
        return t

    if key_major:
        for j in range(N_HEADS // 2):
            rt = (prep(x[:, j * LANES:(j + 1) * LANES], gq_ref[...]) * qscale).T
            q_ref[0, 2 * j] = rt[:HEAD_DIM].astype(q_ref.dtype)
            q_ref[0, 2 * j + 1] = rt[HEAD_DIM:].astype(q_ref.dtype)
        k = prep(x[:, GROUP_WIDTH:GROUP_WIDTH + KV_WIDTH], gk_ref[...])
        kt_ref[0, 0] = k[:, :HEAD_DIM].astype(kt_ref.dtype)
        kt_ref[0, 1] = k[:, HEAD_DIM:].astype(kt_ref.dtype)
        vt = x[:, GROUP_WIDTH + KV_WIDTH:].T
        extra = VT_ROWS - HEAD_DIM
        tail = (lax.broadcasted_iota(jnp.int32, (extra, vt.shape[1]), 0) == 0).astype(v_ref.dtype)
        for kv in range(KV_HEADS):
            v_ref[0, kv, 0:HEAD_DIM] = vt[kv * HEAD_DIM:(kv + 1) * HEAD_DIM].astype(v_ref.dtype)
            v_ref[0, kv, HEAD_DIM:] = tail
        return
    for j in range(N_HEADS // 2):
        r = prep(x[:, j * LANES:(j + 1) * LANES], gq_ref[...]) * qscale
        q_ref[0, 2 * j] = r[:, :HEAD_DIM].astype(q_ref.dtype)
        q_ref[0, 2 * j + 1] = r[:, HEAD_DIM:].astype(q_ref.dtype)
    kt = prep(x[:, GROUP_WIDTH:GROUP_WIDTH + KV_WIDTH], gk_ref[...]).T
    kt_ref[0, 0] = kt[:HEAD_DIM].astype(kt_ref.dtype)
    kt_ref[0, 1] = kt[HEAD_DIM:].astype(kt_ref.dtype)
    v = x[:, GROUP_WIDTH + KV_WIDTH:]
    lane = lax.broadcasted_iota(jnp.int32, v.shape, 1)
    ones_col = (lane == HEAD_DIM).astype(F32)
    v_ref[0, 0] = jnp.where(lane < HEAD_DIM, v, ones_col).astype(v_ref.dtype)
    v_ref[0, 1] = jnp.where(lane < HEAD_DIM, pltpu.roll(v, HEAD_DIM, 1), ones_col).astype(v_ref.dtype)


def _qkv_prep(p, col, tabs, gq, gk, bd, *, norm, rope, qscale, key_major=False):
    nb, lx, _ = p.shape
    tl = min(ROW_TILE, lx)
    cos, sa, sb = tabs
    kern = functools.partial(_qkv_prep_kernel, norm=norm, rope=rope, qscale=qscale, key_major=key_major)
    tab_spec = pl.BlockSpec((tl, LANES), lambda b, i: (i, 0))
    row_spec = pl.BlockSpec((1, LANES), lambda b, i: (0, 0))
    if key_major:
        out_specs = [pl.BlockSpec((1, N_HEADS, HEAD_DIM, tl), lambda b, i: (b, 0, 0, i)),
                     pl.BlockSpec((1, KV_HEADS, tl, HEAD_DIM), lambda b, i: (b, 0, i, 0)),
                     pl.BlockSpec((1, KV_HEADS, VT_ROWS, tl), lambda b, i: (b, 0, 0, i))]
        out_shape = [jax.ShapeDtypeStruct((nb, N_HEADS, HEAD_DIM, lx), MXU_DTYPE),
                     jax.ShapeDtypeStruct((nb, KV_HEADS, lx, HEAD_DIM), MXU_DTYPE),
                     jax.ShapeDtypeStruct((nb, KV_HEADS, VT_ROWS, lx), MXU_DTYPE)]
    else:
        out_specs = [pl.BlockSpec((1, N_HEADS, tl, HEAD_DIM), lambda b, i: (b, 0, i, 0)),
                     pl.BlockSpec((1, KV_HEADS, HEAD_DIM, tl), lambda b, i: (b, 0, 0, i)),
                     pl.BlockSpec((1, KV_HEADS, tl, LANES), lambda b, i: (b, 0, i, 0))]
        out_shape = [jax.ShapeDtypeStruct((nb, N_HEADS, lx, HEAD_DIM), MXU_DTYPE),
                     jax.ShapeDtypeStruct((nb, KV_HEADS, HEAD_DIM, lx), MXU_DTYPE),
                     jax.ShapeDtypeStruct((nb, KV_HEADS, lx, LANES), MXU_DTYPE)]
    return pl.pallas_call(
        kern,
        grid=(nb, lx // tl),
        in_specs=[pl.BlockSpec((1, tl, IN_TILE), lambda b, i: (b, i, col)),
                  tab_spec, tab_spec, tab_spec, row_spec, row_spec,
                  pl.BlockSpec((LANES, LANES), lambda b, i: (0, 0))],
        out_specs=out_specs,
        out_shape=out_shape,
        compiler_params=_cparams("parallel", "parallel"),
        name="qkv_prep",
    )(p, cos, sa, sb, gq, gk, bd)


def _flash_kernel(*refs, has_sink, tq, tk, nk):
    if has_sink:
        q_ref, kt_ref, v_ref, sink_ref, o_ref, s_sc, p_sc, a_sc, m_sc, acc_sc = refs
    else:
        q_ref, kt_ref, v_ref, o_ref, s_sc, p_sc, a_sc, m_sc, acc_sc = refs
    rows = Q_PER_KV * tq
    if has_sink:
        m_sc[...] = sink_ref[0] * LOG2E
        lane = lax.broadcasted_iota(jnp.int32, acc_sc.shape, 1)
        acc_sc[...] = (lane == HEAD_DIM).astype(F32)
    else:
        m_sc[...] = jnp.full_like(m_sc, -jnp.inf)
        acc_sc[...] = jnp.zeros_like(acc_sc)
    q = q_ref[0].reshape(rows, HEAD_DIM)

    def scores(j, slot):
        off = pl.multiple_of(j * tk, tk)
        s_sc[slot] = jnp.dot(q, kt_ref[0, 0, :, pl.ds(off, tk)], preferred_element_type=F32)

    def softmax(slot):
        s = s_sc[slot]
        m_prev = m_sc[...]
        m_new = jnp.maximum(m_prev, jnp.max(s, axis=1, keepdims=True))
        a_sc[slot] = jnp.exp2(m_prev - m_new)
        p_sc[slot] = jnp.exp2(s - m_new[:, :1]).astype(p_sc.dtype)
        m_sc[...] = m_new

    def values(j, slot):
        off = pl.multiple_of(j * tk, tk)
        pv = jnp.dot(p_sc[slot], v_ref[0, 0, pl.ds(off, tk), :], preferred_element_type=F32)
        acc_sc[...] = acc_sc[...] * a_sc[slot] + pv

    def stage(j, slot, n_after):
        values(j, slot)
        if n_after >= 1:
            softmax(1 - slot)
        if n_after >= 2:
            scores(j + 2, slot)

    scores(0, 0)
    softmax(0)
    if nk > 1:
        scores(1, 1)
    pairs = max(nk - 2, 0) // 2

    def body(jj, carry):
        stage(2 * jj, 0, 2)
        stage(2 * jj + 1, 1, 2)
        return carry

    lax.fori_loop(0, pairs, body, 0)
    for j in range(2 * pairs, nk):
        stage(j, j % 2, nk - 1 - j)

    acc = acc_sc[...]
    o = acc[:, :HEAD_DIM] / acc[:, HEAD_DIM:HEAD_DIM + 1]
    for h in range(Q_PER_KV):
        o_ref[0, :, h * HEAD_DIM:(h + 1) * HEAD_DIM] = o[h * tq:(h + 1) * tq].astype(o_ref.dtype)


def _flash(q, kt, v, sink_rows=None, *, tq, tk):
    nb, _, lq, _ = q.shape
    lk = kt.shape[-1]
    rows = Q_PER_KV * tq
    has_sink = sink_rows is not None
    in_specs = [pl.BlockSpec((1, Q_PER_KV, tq, HEAD_DIM), lambda b, k, i: (b, k, i, 0)),
                pl.BlockSpec((1, 1, HEAD_DIM, lk), lambda b, k, i: (b, k, 0, 0)),
                pl.BlockSpec((1, 1, lk, LANES), lambda b, k, i: (b, k, 0, 0))]
    args = [q, kt, v]
    if has_sink:
        in_specs.append(pl.BlockSpec((1, rows, LANES), lambda b, k, i: (k, 0, 0)))
        args.append(sink_rows)
    return pl.pallas_call(
        functools.partial(_flash_kernel, has_sink=has_sink, tq=tq, tk=tk, nk=lk // tk),
        grid=(nb, KV_HEADS, lq // tq),
        in_specs=in_specs,
        out_specs=pl.BlockSpec((1, tq, Q_PER_KV * HEAD_DIM), lambda b, k, i: (b, i, k)),
        out_shape=jax.ShapeDtypeStruct((nb, lq, GROUP_WIDTH), MXU_DTYPE),
        scratch_shapes=[pltpu.VMEM((2, rows, tk), F32), pltpu.VMEM((2, rows, tk), MXU_DTYPE),
                        pltpu.VMEM((2, rows, LANES), F32), pltpu.VMEM((rows, LANES), F32),
                        pltpu.VMEM((rows, LANES), F32)],
        compiler_params=_cparams("parallel", "parallel", "parallel"),
        name="flash_attn",
    )(*args)


def _window_km_kernel(qt_ref, kp_ref, kc_ref, kn_ref, kx_ref, vp_ref, vc_ref, vn_ref, vx_ref, sink_ref, o_ref):
    i = pl.program_id(2)
    nblk = pl.num_programs(2)
    w = WINDOW
    qb = qt_ref.shape[3]
    qt = jnp.concatenate([qt_ref[0, h] for h in range(Q_PER_KV)], axis=1)
    kb = jnp.concatenate([kp_ref[0, 0], kc_ref[0, 0], kn_ref[0, 0], kx_ref[0, 0]], axis=0)
    s = jnp.dot(kb, qt, preferred_element_type=F32)
    cols = Q_PER_KV * qb
    rel = (lax.broadcasted_iota(jnp.int32, (qb, cols), 0)
           - lax.broadcasted_iota(jnp.int32, (qb, cols), 1) % qb)
    lo = jnp.where(i > 0, 0, 2 * qb)
    hi = jnp.where(i < nblk - 1, w - qb, -2 * qb)
    sp = jnp.where(rel[:w] >= lo, s[0:w], NEG_INF)
    sc = s[w:w + qb]
    if qb > w:
        sc = jnp.where(rel >= -w, jnp.where(rel <= w, sc, NEG_INF), NEG_INF)
    sn = jnp.where(rel[:w] <= hi, s[w + qb:2 * w + qb], NEG_INF)
    sx = s[2 * w + qb:]
    sink = sink_ref[0] * LOG2E
    m = jnp.maximum(jnp.maximum(jnp.max(sp, axis=0, keepdims=True), jnp.max(sc, axis=0, keepdims=True)),
                    jnp.maximum(jnp.max(sn, axis=0, keepdims=True), jnp.max(sx, axis=0, keepdims=True)))
    m = jnp.maximum(m, sink)
    dt = vp_ref.dtype
    pt = jnp.concatenate([jnp.exp2(sp - m).astype(dt), jnp.exp2(sc - m).astype(dt),
                          jnp.exp2(sn - m).astype(dt), jnp.exp2(sx - m).astype(dt)], axis=0)
    vb = jnp.concatenate([vp_ref[0, 0], vc_ref[0, 0], vn_ref[0, 0], vx_ref[0, 0]], axis=1)
    acc = jnp.dot(vb, pt, preferred_element_type=F32)
    ot = acc[:HEAD_DIM] / (acc[HEAD_DIM:HEAD_DIM + 1] + jnp.exp2(sink - m))
    for h in range(Q_PER_KV):
        o_ref[0, :, h * HEAD_DIM:(h + 1) * HEAD_DIM] = ot[:, h * qb:(h + 1) * qb].T.astype(o_ref.dtype)


def _window_km(qt, k, vt, kx, vtx, sink_lanes, *, qb):
    nb, _, _, lq = qt.shape
    lc = kx.shape[2]
    nblk = lq // qb
    r = qb // WINDOW
    nw = lq // WINDOW

    def prev(i):
        return jnp.maximum(i * r - 1, 0)

    def nxt(i):
        return jnp.minimum((i + 1) * r, nw - 1)

    return pl.pallas_call(
        _window_km_kernel,
        grid=(nb, KV_HEADS, nblk),
        in_specs=[pl.BlockSpec((1, Q_PER_KV, HEAD_DIM, qb), lambda b, kv, i: (b, kv, 0, i)),
                  pl.BlockSpec((1, 1, WINDOW, HEAD_DIM), lambda b, kv, i: (b, kv, prev(i), 0)),
                  pl.BlockSpec((1, 1, qb, HEAD_DIM), lambda b, kv, i: (b, kv, i, 0)),
                  pl.BlockSpec((1, 1, WINDOW, HEAD_DIM), lambda b, kv, i: (b, kv, nxt(i), 0)),
                  pl.BlockSpec((1, 1, lc, HEAD_DIM), lambda b, kv, i: (b, kv, 0, 0)),
                  pl.BlockSpec((1, 1, VT_ROWS, WINDOW), lambda b, kv, i: (b, kv, 0, prev(i))),
                  pl.BlockSpec((1, 1, VT_ROWS, qb), lambda b, kv, i: (b, kv, 0, i)),
                  pl.BlockSpec((1, 1, VT_ROWS, WINDOW), lambda b, kv, i: (b, kv, 0, nxt(i))),
                  pl.BlockSpec((1, 1, VT_ROWS, lc), lambda b, kv, i: (b, kv, 0, 0)),
                  pl.BlockSpec((1, 1, Q_PER_KV * qb), lambda b, kv, i: (kv, 0, 0))],
        out_specs=pl.BlockSpec((1, qb, Q_PER_KV * HEAD_DIM), lambda b, kv, i: (b, i, kv)),
        out_shape=jax.ShapeDtypeStruct((nb, lq, GROUP_WIDTH), MXU_DTYPE),
        compiler_params=_cparams("parallel", "parallel", "parallel"),
        name="window_attn_km",
    )(qt, k, k, k, kx, vt, vt, vt, vtx, sink_lanes)


def _retention_kernel(*refs, reverse, rope, final):
    refs = list(refs)
    q_ref, k_ref, v_ref = refs[:3]
    del refs[:3]
    if rope:
        cos_ref, sa_ref, sb_ref = refs[:3]
        del refs[:3]
    lg_ref, lgw_ref, s0_ref = refs[:3]
    del refs[:3]
    if final:
        of_ref, gate_ref, gn_ref = refs[:3]
        del refs[:3]
    o_ref, sfin_ref, s_sc, d_sc = refs
    c = CHUNK
    n = pl.program_id(0)
    nb = q_ref.shape[0]
    pairs = N_HEADS // 2
    lane = lax.broadcasted_iota(jnp.int32, (c, LANES), 1)
    first = lane < HEAD_DIM
    blk = (lax.broadcasted_iota(jnp.int32, (LANES, LANES), 0) < HEAD_DIM) == (
        lax.broadcasted_iota(jnp.int32, (LANES, LANES), 1) < HEAD_DIM)

    @pl.when(n == 0)
    def _():
        for b in range(nb):
            for t in range(pairs):
                s_sc[b, t] = jnp.zeros((LANES, LANES), F32)
                s_sc[b, t, 0:HEAD_DIM, 0:HEAD_DIM] = s0_ref[b, 2 * t]
                s_sc[b, t, HEAD_DIM:, HEAD_DIM:] = s0_ref[b, 2 * t + 1]
        ri = lax.broadcasted_iota(jnp.int32, (c, c), 0)
        ci = lax.broadcasted_iota(jnp.int32, (c, c), 1)
        rel = (ci - ri if reverse else ri - ci).astype(F32)
        for h in range(N_HEADS):
            ld = _log_sigmoid(lg_ref[h])
            d_sc[h] = jnp.where(rel >= 0, jnp.exp(jnp.maximum(rel, 0.0) * ld), 0.0)

    ldw = _log_sigmoid(lgw_ref[...])
    idx = lax.broadcasted_iota(jnp.int32, (c, GROUP_WIDTH), 0).astype(F32)
    if reverse:
        q_scale = jnp.exp((c - idx) * ldw)
        k_scale = jnp.exp(idx * ldw)
    else:
        q_scale = jnp.exp((idx + 1.0) * ldw)
        k_scale = jnp.exp((c - 1.0 - idx) * ldw)
    chunk_decay = jnp.exp(c * ldw)

    for b in range(nb):
        q, k, v = q_ref[b], k_ref[b], v_ref[b]
        outs = []
        for t in range(pairs):
            sl = slice(t * LANES, (t + 1) * LANES)
            qp, kp, vp = q[:, sl], k[:, sl], v[:, sl]
            if rope:
                qp = _rope_apply(qp, cos_ref[...], sa_ref[...], sb_ref[...], HEAD_DIM // 2)
                kp = _rope_apply(kp, cos_ref[...], sa_ref[...], sb_ref[...], HEAD_DIM // 2)
            kp = kp * (HEAD_DIM ** -0.5)
            kb, vb = kp.astype(MXU_DTYPE), vp.astype(MXU_DTYPE)
            s0 = _mm_nt(jnp.where(first, qp, 0.0), kb) * d_sc[2 * t]
            s1 = _mm_nt(jnp.where(first, 0.0, qp), kb) * d_sc[2 * t + 1]
            o = jnp.where(first, _mm(s0, vb), _mm(s1, vb))
            o = o + _mm(qp * q_scale[:, sl], s_sc[b, t])
            kv = lax.dot_general((kp * k_scale[:, sl]).astype(MXU_DTYPE), vb,
                                 (((0,), (0,)), ((), ())), preferred_element_type=F32)
            s_sc[b, t] = s_sc[b, t] * chunk_decay[:, sl] + jnp.where(blk, kv, 0.0)
            if final:
                o = o + of_ref[b][:, sl]
                inv = 1.0 / HEAD_DIM
                mu0 = jnp.sum(jnp.where(first, o, 0.0), axis=1, keepdims=True) * inv
                mu1 = jnp.sum(jnp.where(first, 0.0, o), axis=1, keepdims=True) * inv
                dlt = o - jnp.where(first, mu0, mu1)
                sq = dlt * dlt
                var0 = jnp.sum(jnp.where(first, sq, 0.0), axis=1, keepdims=True) * inv
                var1 = jnp.sum(jnp.where(first, 0.0, sq), axis=1, keepdims=True) * inv
                o = dlt * lax.rsqrt(jnp.where(first, var0, var1) + EPS)
            outs.append(o)
        o = jnp.concatenate(outs, axis=1)
        if final:
            o = o * gn_ref[...] * _silu(gate_ref[b])
        o_ref[b] = o.astype(o_ref.dtype)

    @pl.when(n == pl.num_programs(0) - 1)
    def _():
        for b in range(nb):
            for t in range(pairs):
                sfin_ref[b, 2 * t] = s_sc[b, t, 0:HEAD_DIM, 0:HEAD_DIM]
                sfin_ref[b, 2 * t + 1] = s_sc[b, t, HEAD_DIM:, HEAD_DIM:]


def _retention(p, tabs, lg, lgw, s0, *, reverse, o_fwd=None, gn=None):
    nb, lx, _ = p.shape
    c = CHUNK
    nc = lx // c
    rope = tabs is not None
    final = o_fwd is not None

    def cm(n):
        return nc - 1 - n if reverse else n

    def col(j):
        return pl.BlockSpec((nb, c, GROUP_WIDTH), lambda n: (0, cm(n), j))

    state = pl.BlockSpec((nb, N_HEADS, HEAD_DIM, HEAD_DIM), lambda n: (0, 0, 0, 0))
    in_specs = [col(COL_RET), col(COL_RET + 1), col(COL_RET + 2)]
    args = [p, p, p]
    if rope:
        in_specs += [pl.BlockSpec((c, LANES), lambda n: (cm(n), 0))] * 3
        args += list(tabs)
    in_specs += [pl.BlockSpec((N_HEADS, 1, LANES), lambda n: (0, 0, 0)),
                 pl.BlockSpec((1, GROUP_WIDTH), lambda n: (0, 0)), state]
    args += [lg, lgw, s0]
    if final:
        in_specs += [col(0), col(COL_RET + 3), pl.BlockSpec((1, GROUP_WIDTH), lambda n: (0, 0))]
        args += [o_fwd, p, gn]
    return pl.pallas_call(
        functools.partial(_retention_kernel, reverse=reverse, rope=rope, final=final),
        grid=(nc,),
        in_specs=in_specs,
        out_specs=[col(0), state],
        out_shape=[jax.ShapeDtypeStruct((nb, lx, GROUP_WIDTH), MXU_DTYPE if final else F32),
                   jax.ShapeDtypeStruct((nb, N_HEADS, HEAD_DIM, HEAD_DIM), F32)],
        scratch_shapes=[pltpu.VMEM((nb, N_HEADS // 2, LANES, LANES), F32), pltpu.VMEM((N_HEADS, c, c), F32)],
        compiler_params=_cparams("arbitrary"),
        name="retention",
    )(*args)


def _ssd_conv_kernel(xp_ref, xc_ref, xn_ref, w_ref, b_ref, o_ref, x_sc):
    i = pl.program_id(1)
    tl = xc_ref.shape[1]
    x_sc[0:HALO] = jnp.where(i > 0, xp_ref[0], 0.0)
    x_sc[HALO:HALO + tl] = xc_ref[0]
    x_sc[HALO + tl:] = jnp.where(i < pl.num_programs(1) - 1, xn_ref[0], 0.0)
    acc = jnp.zeros((tl, xc_ref.shape[2]), F32) + b_ref[...]
    for k in range(SSD_CONV):
        off = HALO + k - SSD_CONV // 2
        acc = acc + x_sc[off:off + tl] * w_ref[k:k + 1]
    o_ref[0] = _silu(acc)


def _ssd_conv(p, w, b):
    nb, lx, _ = p.shape
    tl = min(ROW_TILE, lx)
    nh = lx // HALO
    r = tl // HALO
    ch = SSD_CONV_CH
    col = COL_XBC
    return pl.pallas_call(
        _ssd_conv_kernel,
        grid=(nb, lx // tl),
        in_specs=[pl.BlockSpec((1, HALO, ch), lambda b_, i: (b_, jnp.maximum(i * r - 1, 0), col)),
                  pl.BlockSpec((1, tl, ch), lambda b_, i: (b_, i, col)),
                  pl.BlockSpec((1, HALO, ch), lambda b_, i: (b_, jnp.minimum((i + 1) * r, nh - 1), col)),
                  pl.BlockSpec((SSD_CONV, ch), lambda b_, i: (0, 0)),
                  pl.BlockSpec((1, ch), lambda b_, i: (0, 0))],
        out_specs=pl.BlockSpec((1, tl, ch), lambda b_, i: (b_, i, 0)),
        out_shape=jax.ShapeDtypeStruct((nb, lx, ch), F32),
        scratch_shapes=[pltpu.VMEM((tl + 2 * HALO, ch), F32)],
        compiler_params=_cparams("parallel", "parallel"),
        name="ssd_conv",
    )(p, p, p, w, b)


def _ssd_kernel(*refs, reverse, final):
    refs = list(refs)
    xbc_ref, dt_ref, dtt_ref, bias_row_ref, alog_row_ref, bias_col_ref, alog_col_ref, h0_ref = refs[:8]
    del refs[:8]
    if final:
        yf_ref, z_ref, dskip_ref, ng_ref = refs[:4]
        del refs[:4]
    y_ref, hfin_ref, h_sc = refs
    c = CHUNK
    n = pl.program_id(0)
    nb = xbc_ref.shape[0]
    hp = lax.Precision.HIGHEST

    @pl.when(n == 0)
    def _():
        h_sc[...] = h0_ref[...]

    ri = lax.broadcasted_iota(jnp.int32, (c, c), 0)
    ci = lax.broadcasted_iota(jnp.int32, (c, c), 1)
    tri_b = (ri <= ci) if reverse else (ri >= ci)
    tri = tri_b.astype(F32)
    ones = jnp.ones((c, c), F32)

    for b in range(nb):
        dt = _softplus(dt_ref[b] + bias_row_ref[...])
        a = dt * (-jnp.exp(alog_row_ref[...]))
        a_cum = jnp.dot(tri, a, precision=hp, preferred_element_type=F32)
        dtt = _softplus(dtt_ref[b] + bias_col_ref[...])
        at = dtt * (-jnp.exp(alog_col_ref[...]))
        a_cum_t = lax.dot_general(at, tri, (((1,), (1,)), ((), ())), precision=hp, preferred_element_type=F32)
        a_tot_t = jnp.dot(at, ones, precision=hp, preferred_element_type=F32)

        xbc = xbc_ref[b]
        x = xbc[:, :GROUP_WIDTH]
        xt = x.T
        outs = []
        for g in range(SSD_GROUPS):
            bm = xbc[:, GROUP_WIDTH + g * SSD_STATE:GROUP_WIDTH + (g + 1) * SSD_STATE]
            cmat = xbc[:, GROUP_WIDTH + SSD_BC + g * SSD_STATE:GROUP_WIDTH + SSD_BC + (g + 1) * SSD_STATE]
            cb = _mm_nt(cmat, bm)
            for hh in range(SSD_HEADS // SSD_GROUPS):
                h = g * (SSD_HEADS // SSD_GROUPS) + hh
                sl = slice(h * 64, (h + 1) * 64)
                col = a_cum[:, h:h + 1]
                row = a_cum_t[h:h + 1, :]
                dt_row = dtt[h:h + 1, :]
                tot_row = a_tot_t[h:h + 1, :]
                decay = jnp.exp(jnp.where(tri_b, col - row, NEG_INF))
                w = cb * decay * dt_row
                y = _mm(w, x[:, sl]) + _mm_nt(cmat, h_sc[b, h]) * jnp.exp(col)
                to_end = jnp.exp(tot_row - row) * dt_row
                st = _mm(xt[sl, :] * to_end, bm)
                h_sc[b, h] = h_sc[b, h] * jnp.exp(tot_row) + st
                outs.append(y)
        y = jnp.concatenate(outs, axis=1)
        if final:
            y = (yf_ref[b] + y + dskip_ref[...] * x) * _silu(z_ref[b])
            half = GROUP_WIDTH // SSD_GROUPS
            y = jnp.concatenate([_rms(y[:, :half]), _rms(y[:, half:])], axis=1) * ng_ref[...]
        y_ref[b] = y.astype(y_ref.dtype)

    @pl.when(n == pl.num_programs(0) - 1)
    def _():
        hfin_ref[...] = h_sc[...]


def _ssd(xbc, p, dtt, prm, h0, *, reverse, y_fwd=None, dskip=None, ng=None):
    nb, lx, _ = xbc.shape
    c = CHUNK
    nc = lx // c
    final = y_fwd is not None

    def cm(n):
        return nc - 1 - n if reverse else n

    row = pl.BlockSpec((1, LANES), lambda n: (0, 0))
    colb = pl.BlockSpec((SSD_HEADS, LANES), lambda n: (0, 0))
    wide = pl.BlockSpec((1, GROUP_WIDTH), lambda n: (0, 0))
    state = pl.BlockSpec((nb, SSD_HEADS, 64, SSD_STATE), lambda n: (0, 0, 0, 0))
    yspec = pl.BlockSpec((nb, c, GROUP_WIDTH), lambda n: (0, cm(n), 0))
    in_specs = [pl.BlockSpec((nb, c, SSD_CONV_CH), lambda n: (0, cm(n), 0)),
                pl.BlockSpec((nb, c, LANES), lambda n: (0, cm(n), COL_DT)),
                pl.BlockSpec((nb, SSD_HEADS, c), lambda n: (0, 0, cm(n))),
                row, row, colb, colb, state]
    args = [xbc, p, dtt, *prm, h0]
    if final:
        in_specs += [yspec, pl.BlockSpec((nb, c, GROUP_WIDTH), lambda n: (0, cm(n), COL_SZ)), wide, wide]
        args += [y_fwd, p, dskip, ng]
    return pl.pallas_call(
        functools.partial(_ssd_kernel, reverse=reverse, final=final),
        grid=(nc,),
        in_specs=in_specs,
        out_specs=[yspec, state],
        out_shape=[jax.ShapeDtypeStruct((nb, lx, GROUP_WIDTH), MXU_DTYPE if final else F32),
                   jax.ShapeDtypeStruct((nb, SSD_HEADS, 64, SSD_STATE), F32)],
        scratch_shapes=[pltpu.VMEM((nb, SSD_HEADS, 64, SSD_STATE), F32)],
        compiler_params=_cparams("arbitrary"),
        name="ssd_scan",
    )(*args)


def _outproj_kernel(a_ref, w_ref_in, r_ref, s_ref, x_ref, g_ref, w_ref, o_ref):
    acc = jnp.zeros(o_ref.shape[1:], F32)
    for m, ref in enumerate((a_ref, w_ref_in, r_ref, s_ref)):
        acc = acc + jnp.dot(ref[0], w_ref[m * GROUP_WIDTH:(m + 1) * GROUP_WIDTH, :], preferred_element_type=F32)
    o_ref[0] = x_ref[0] + g_ref[0] * acc


def _out_proj(mix, x, g1, w):
    nb, lx, d = x.shape
    tm = min(TOKEN_TM, lx)
    mspec = pl.BlockSpec((1, tm, GROUP_WIDTH), lambda b, i: (b, i, 0))
    return pl.pallas_call(
        _outproj_kernel,
        grid=(nb, lx // tm),
        in_specs=[mspec, mspec, mspec, mspec,
                  pl.BlockSpec((1, tm, d), lambda b, i: (b, i, 0)),
                  pl.BlockSpec((1, 1, d), lambda b, i: (b, 0, 0)),
                  pl.BlockSpec((4 * GROUP_WIDTH, d), lambda b, i: (0, 0))],
        out_specs=pl.BlockSpec((1, tm, d), lambda b, i: (b, i, 0)),
        out_shape=jax.ShapeDtypeStruct((nb, lx, d), F32),
        compiler_params=_cparams("parallel", "parallel"),
        name="out_proj",
    )(*mix, x, g1, w)


def _ffn_kernel(*refs, final):
    refs = list(refs)
    (xp_ref, xc_ref, xn_ref, g_ref, sc_ref, sh_ref, gate_ref, wg_ref, wv_ref, cwg_ref, cwv_ref, cbg_ref, cbv_ref,
     wd_ref) = refs[:14]
    del refs[:14]
    if final:
        fg_ref = refs.pop(0)
    o_ref, h_sc, acc_sc, ug_sc, uv_sc = refs
    i, j = pl.program_id(1), pl.program_id(2)
    tm = xc_ref.shape[1]

    @pl.when(j == 0)
    def _():
        def nm(t):
            return (_rms(t) * g_ref[...] * (1.0 + sc_ref[0]) + sh_ref[0]).astype(h_sc.dtype)
        h_sc[0:HALO] = nm(xp_ref[0])
        h_sc[HALO:HALO + tm] = nm(xc_ref[0])
        h_sc[HALO + tm:] = nm(xn_ref[0])
        acc_sc[...] = jnp.zeros_like(acc_sc)

    h = h_sc[...]
    keep_prev = jnp.where(i > 0, 1.0, 0.0)
    keep_next = jnp.where(i < pl.num_programs(1) - 1, 1.0, 0.0)
    parts = ug_sc.shape[0]
    th = ug_sc.shape[2]

    def conv(u_ref, w_ref, b_ref, cs):
        acc = jnp.zeros((tm, th), F32) + b_ref[:, cs]
        for k in range(FFN_CONV):
            off = HALO + k - FFN_CONV // 2
            acc = acc + u_ref[off:off + tm] * w_ref[k:k + 1, cs]
        return acc

    down = None
    for a in range(parts):
        cs = slice(a * th, (a + 1) * th)
        for w_ref, u_sc in ((wg_ref, ug_sc), (wv_ref, uv_sc)):
            u_sc[a] = jnp.dot(h, w_ref[:, cs], preferred_element_type=F32)
            u_sc[a, 0:HALO] = u_sc[a, 0:HALO] * keep_prev
            u_sc[a, HALO + tm:] = u_sc[a, HALO + tm:] * keep_next
    for a in range(parts):
        cs = slice(a * th, (a + 1) * th)
        act = _silu(conv(ug_sc.at[a], cwg_ref, cbg_ref, cs)) * conv(uv_sc.at[a], cwv_ref, cbv_ref, cs)
        d = jnp.dot(act.astype(wd_ref.dtype), wd_ref[cs, :], preferred_element_type=F32)
        down = d if down is None else down + d
    acc_sc[...] += down

    @pl.when(j == pl.num_programs(2) - 1)
    def _():
        y = xc_ref[0] + gate_ref[0] * acc_sc[...]
        if final:
            y = _rms(y) * fg_ref[...]
        o_ref[0] = y


def _ffn(x, g, sc, sh, gate, w_up, conv_w, conv_b, w_down, final_g=None):
    nb, lx, d = x.shape
    f = w_down.shape[0]
    tm = min(TOKEN_TM, lx)
    tf = FFN_TF
    nf = f // tf
    nh = lx // HALO
    r = tm // HALO
    final = final_g is not None
    vec = pl.BlockSpec((1, d), lambda b, i, j: (0, 0))
    mod = pl.BlockSpec((1, 1, d), lambda b, i, j: (b, 0, 0))
    in_specs = [pl.BlockSpec((1, HALO, d), lambda b, i, j: (b, jnp.maximum(i * r - 1, 0), 0)),
                pl.BlockSpec((1, tm, d), lambda b, i, j: (b, i, 0)),
                pl.BlockSpec((1, HALO, d), lambda b, i, j: (b, jnp.minimum((i + 1) * r, nh - 1), 0)),
                vec, mod, mod, mod,
                pl.BlockSpec((d, tf), lambda b, i, j: (0, j)),
                pl.BlockSpec((d, tf), lambda b, i, j: (0, nf + j)),
                pl.BlockSpec((FFN_CONV, tf), lambda b, i, j: (0, j)),
                pl.BlockSpec((FFN_CONV, tf), lambda b, i, j: (0, nf + j)),
                pl.BlockSpec((1, tf), lambda b, i, j: (0, j)),
                pl.BlockSpec((1, tf), lambda b, i, j: (0, nf + j)),
                pl.BlockSpec((tf, d), lambda b, i, j: (j, 0))]
    args = [x, x, x, g, sc, sh, gate, w_up, w_up, conv_w, conv_w, conv_b, conv_b, w_down]
    if final:
        in_specs.append(vec)
        args.append(final_g)
    return pl.pallas_call(
        functools.partial(_ffn_kernel, final=final),
        grid=(nb, lx // tm, nf),
        in_specs=in_specs,
        out_specs=pl.BlockSpec((1, tm, d), lambda b, i, j: (b, i, 0)),
        out_shape=jax.ShapeDtypeStruct((nb, lx, d), F32),
        scratch_shapes=[pltpu.VMEM((tm + 2 * HALO, d), MXU_DTYPE), pltpu.VMEM((tm, d), F32),
                        pltpu.VMEM((FFN_PARTS, tm + 2 * HALO, tf // FFN_PARTS), F32),
                        pltpu.VMEM((FFN_PARTS, tm + 2 * HALO, tf // FFN_PARTS), F32)],
        compiler_params=_cparams("parallel", "parallel", "arbitrary"),
        name="conv_ffn",
    )(*args)


def _pair_tables(ang_blocks):
    cos, sa, sb = [], [], []
    for ang in ang_blocks:
        c, s = jnp.cos(ang), jnp.sin(ang)
        z = jnp.zeros_like(s)
        cos += [c, c]
        sa += [-s, z]
        sb += [z, s]
    def two(parts):
        t = jnp.concatenate(parts, axis=-1)
        return jnp.concatenate([t, t], axis=-1)
    return two(cos), two(sa), two(sb)


def _lane_rep(v, width):
    return jnp.repeat(v, width)[None, :]


def kernel(x, c, ctx, c_ctx, w_ada, b_ada, norm1_g, w_in, attn_qn_g, attn_kn_g, win_sink, ret_decay_logit,
           ret_gn_g, ssd_conv_w, ssd_conv_b, ssd_a_log, ssd_dt_bias, ssd_d, ssd_norm_g, w_out, norm2_g,
           ffn_w_up, ffn_conv_w, ffn_conv_b, ffn_w_down, final_g):
    b, l, d = x.shape
    lc = ctx.shape[1]
    depth = w_ada.shape[0]
    assert d == D_MODEL and l % 512 == 0 and lc % 256 == 0 and b + 1 <= 8

    rows = l // GRID_W
    row = jnp.repeat(jnp.arange(rows), GRID_W).astype(F32)
    colp = jnp.tile(jnp.arange(GRID_W), rows).astype(F32)
    n_ax = HEAD_DIM // 4
    inv_ax = ROPE_THETA ** (-jnp.arange(n_ax, dtype=F32) / n_ax)
    tabs_2d = _pair_tables([row[:, None] * inv_ax, colp[:, None] * inv_ax])
    n_ret = HEAD_DIM // 2
    inv_ret = RET_THETA ** (-jnp.arange(n_ret, dtype=F32) / n_ret)
    tabs_ret = _pair_tables([jnp.arange(l, dtype=F32)[:, None] * inv_ret])
    tabs_ctx = (jnp.ones((lc, LANES), F32), jnp.zeros((lc, LANES), F32), jnp.zeros((lc, LANES), F32))
    bd = (jnp.arange(LANES)[:, None] // HEAD_DIM == jnp.arange(LANES)[None, :] // HEAD_DIM).astype(jnp.bfloat16)

    cvec = jnp.zeros((8, d), F32).at[:b].set(c).at[b].set(c_ctx)
    mod_all = _ada(cvec, w_ada, b_ada)

    qscale = HEAD_DIM ** -0.5 * LOG2E
    xc = ctx
    for i in range(depth):
        need_ctx = i < depth - 1
        mod = mod_all[i].reshape(8, 6, 1, d)
        sh1, sc1, g1, sh2, sc2, g2 = [mod[:b, j] for j in range(6)]
        sh1c, sc1c, g1c, sh2c, sc2c, g2c = [jnp.broadcast_to(mod[b:b + 1, j], (b, 1, d)) for j in range(6)]

        w_in_p = jnp.pad(w_in[i], ((0, 0), (0, IN_COLS_PAD - IN_COLS))).astype(MXU_DTYPE)
        n1 = norm1_g[i][None, :]
        p = _in_proj(x, n1, sc1, sh1, w_in_p)
        pc = _in_proj(xc, n1, sc1c, sh1c, w_in_p)

        gq = jnp.tile(attn_qn_g[i], 2)[None, :]
        gk = jnp.tile(attn_kn_g[i], 2)[None, :]
        qa, kta, va = _qkv_prep(p, 0, tabs_2d, gq, gk, bd, norm=True, rope=True, qscale=qscale)
        qac, ktac, vac = _qkv_prep(pc, 0, tabs_ctx, gq, gk, bd, norm=True, rope=False, qscale=qscale)
        qwt, kw, vwt = _qkv_prep(p, 1, tabs_2d, gq, gk, bd, norm=False, rope=True, qscale=qscale, key_major=True)
        _, kwc, vwtc = _qkv_prep(pc, 1, tabs_ctx, gq, gk, bd, norm=False, rope=False, qscale=qscale,
                                 key_major=True)
        if need_ctx:
            qwc, ktwc, vwc = _qkv_prep(pc, 1, tabs_ctx, gq, gk, bd, norm=False, rope=False, qscale=qscale)
        sink = win_sink[i].reshape(KV_HEADS, Q_PER_KV)

        def sink_rows(t):
            return jnp.broadcast_to(sink[:, :, None, None], (KV_HEADS, Q_PER_KV, t, LANES)).reshape(
                KV_HEADS, Q_PER_KV * t, LANES)

        n_keys = l + lc
        tk = n_keys // FLASH_KEY_BLOCKS if n_keys % (FLASH_KEY_BLOCKS * LANES) == 0 else LANES
        o_att = _flash(qa, jnp.concatenate([kta, ktac], axis=-1), jnp.concatenate([va, vac], axis=2),
                       tq=FLASH_TQ, tk=tk)
        sink_lanes = jnp.repeat(sink, WIN_QB, axis=1)[:, None, :]
        o_win = _window_km(qwt, kw, vwt, kwc, vwtc, sink_lanes, qb=WIN_QB)

        lg = jnp.broadcast_to(ret_decay_logit[i][:, :, None, None], (2, N_HEADS, 1, LANES))
        lgw = [_lane_rep(ret_decay_logit[i][dr], HEAD_DIM) for dr in range(2)]
        s0 = jnp.zeros((b, N_HEADS, HEAD_DIM, HEAD_DIM), F32)
        gn = ret_gn_g[i][None, :]
        oc_f, sc_f = _retention(pc, None, lg[0], lgw[0], s0, reverse=False)
        o_ret_c, sc_b = _retention(pc, None, lg[1], lgw[1], s0, reverse=True, o_fwd=oc_f, gn=gn)
        o_f, _ = _retention(p, tabs_ret, lg[0], lgw[0], sc_f, reverse=False)
        o_ret, _ = _retention(p, tabs_ret, lg[1], lgw[1], sc_b, reverse=True, o_fwd=o_f, gn=gn)

        conv_b = ssd_conv_b[i][None, :]
        xbc = _ssd_conv(p, ssd_conv_w[i], conv_b)
        xbcc = _ssd_conv(pc, ssd_conv_w[i], conv_b)
        dtt = jnp.swapaxes(p[:, :, COL_DT * LANES:COL_DT * LANES + SSD_HEADS], 1, 2)
        dttc = jnp.swapaxes(pc[:, :, COL_DT * LANES:COL_DT * LANES + SSD_HEADS], 1, 2)

        def ssd_prm(dr):
            pad = jnp.zeros((LANES - SSD_HEADS,), F32)
            return (jnp.concatenate([ssd_dt_bias[i][dr], pad])[None, :],
                    jnp.concatenate([ssd_a_log[i][dr], pad])[None, :],
                    jnp.broadcast_to(ssd_dt_bias[i][dr][:, None], (SSD_HEADS, LANES)),
                    jnp.broadcast_to(ssd_a_log[i][dr][:, None], (SSD_HEADS, LANES)))

        h0 = jnp.zeros((b, SSD_HEADS, 64, SSD_STATE), F32)
        dskip = _lane_rep(ssd_d[i], 64)
        ng = ssd_norm_g[i][None, :]
        yc_f, hc_f = _ssd(xbcc, pc, dttc, ssd_prm(0), h0, reverse=False)
        o_ssd_c, hc_b = _ssd(xbcc, pc, dttc, ssd_prm(1), h0, reverse=True, y_fwd=yc_f, dskip=dskip, ng=ng)
        y_f, _ = _ssd(xbc, p, dtt, ssd_prm(0), hc_f, reverse=False)
        o_ssd, _ = _ssd(xbc, p, dtt, ssd_prm(1), hc_b, reverse=True, y_fwd=y_f, dskip=dskip, ng=ng)

        w_out_b = w_out[i].astype(MXU_DTYPE)
        w_up_b = ffn_w_up[i].astype(MXU_DTYPE)
        w_down_b = ffn_w_down[i].astype(MXU_DTYPE)
        n2 = norm2_g[i][None, :]
        cb2 = ffn_conv_b[i][None, :]
        x = _out_proj((o_att, o_win, o_ret, o_ssd), x, g1, w_out_b)
        x = _ffn(x, n2, sc2, sh2, g2, w_up_b, ffn_conv_w[i], cb2, w_down_b,
                 final_g=None if need_ctx else final_g[None, :])
        if need_ctx:
            o_att_c = _flash(qac, ktac, vac, tq=lc, tk=lc)
            o_win_c = _flash(qwc, ktwc, vwc, sink_rows(lc), tq=lc, tk=lc)
            xc = _out_proj((o_att_c, o_win_c, o_ret_c, o_ssd_c), xc, g1c, w_out_b)
            xc = _ffn(xc, n2, sc2c, sh2c, g2c, w_up_b, ffn_conv_w[i], cb2, w_down_b)
    return x
```
